```python
import jax, jax.numpy as jnp
from jax import lax
import numpy as np

D_MODEL = 1024
BATCH = 32
SEQ = 2048
DEPTH = 1
DEC_BATCH = 128
DEC_SEQ = 4
PAST_LEN = 8192
PAGE_SIZE = 128

D_HEAD = 64
N_HEADS_TOTAL = D_MODEL // D_HEAD
H_SB = N_HEADS_TOTAL // 2
G_MLP = N_HEADS_TOTAL - H_SB
D_SB = H_SB * D_HEAD
D_MLP = G_MLP * D_HEAD
D_MIX = D_SB + D_MLP
D_IN = 3 * D_SB + 2 * D_MLP
Q_BLOCK = 128
CHUNK = 128
SB_BIAS_INIT = -7.0
N_EXPERTS = 32
TOP_K = 4
D_FF = D_MODEL
SWIGLU_LIMIT = 7.0
SWIGLU_ALPHA = 1.702
EXPERT_BLOCK = 128
N_ADA = 6
EPS = 1e-6

kernel_name = 'stick_breaking_chunkmlp_moe_step'


def rms_norm(x, g):
    xf = x.astype(jnp.float32)
    r = lax.rsqrt(jnp.mean(xf * xf, axis=-1, keepdims=True) + EPS)
    return (xf * r).astype(x.dtype) * g


def ada_terms(c, w_ada, b_ada):
    m = jnp.einsum('bd,de->be', jax.nn.silu(c), w_ada) + b_ada
    return jnp.split(m[:, None, :], N_ADA, axis=-1)


def modulate(x, g, shift, scale):
    return rms_norm(x, g) * (1.0 + scale) + shift


def project(h, w_in):
    p = jnp.einsum('bsd,de->bse', h, w_in)
    B, S = p.shape[:2]
    q, k, v, u, gv = jnp.split(p, [D_SB, 2 * D_SB, 3 * D_SB, 3 * D_SB + D_MLP], axis=-1)
    sb = lambda t: t.reshape(B, S, H_SB, D_HEAD)
    mg = lambda t: jax.nn.gelu(t).reshape(B, S, G_MLP, D_HEAD)
    return sb(q), sb(k), sb(v), mg(u), mg(gv)


def sb_block(q, k, v, b_sb, q_pos, k_pos):
    z = jnp.einsum('bqhd,bkhd->bhqk', q, k).astype(jnp.float32) * (D_HEAD ** -0.5)
    z = z + b_sb.astype(jnp.float32)[None, :, None, None]
    earlier = k_pos[None, :] < q_pos[:, None]
    log_stay = jnp.where(earlier, jax.nn.log_sigmoid(-z), 0.0)
    after = lax.cumsum(log_stay, axis=3, reverse=True) - log_stay
    a = jnp.where(earlier, jnp.exp(jax.nn.log_sigmoid(z) + after), 0.0)
    return jnp.einsum('bhqk,bkhd->bqhd', a.astype(v.dtype), v)


def sb_prompt(q, k, v, b_sb):
    S = q.shape[1]
    outs = []
    for i in range(S // Q_BLOCK):
        lo, hi = i * Q_BLOCK, (i + 1) * Q_BLOCK
        outs.append(sb_block(q[:, lo:hi], k[:, :hi], v[:, :hi], b_sb,
                             jnp.arange(lo, hi), jnp.arange(hi)))
    return jnp.concatenate(outs, axis=1)


def sb_sample(q, k_new, v_new, b_sb, cache_k, cache_v, page_table):
    Bd, T = q.shape[:2]
    past_len = page_table.shape[1] * PAGE_SIZE
    k_past = cache_k[page_table].reshape(Bd, past_len, H_SB, D_HEAD)
    v_past = cache_v[page_table].reshape(Bd, past_len, H_SB, D_HEAD)
    k_all = jnp.concatenate([k_past, k_new.astype(k_past.dtype)], axis=1)
    v_all = jnp.concatenate([v_past, v_new.astype(v_past.dtype)], axis=1)
    return sb_block(q, k_all, v_all, b_sb, past_len + jnp.arange(T), jnp.arange(past_len + T))


def spatial_gate(u, gv, w_s, b_s):
    L = u.shape[2]
    w = jnp.tril(w_s[:, :L, :L])
    mixed = jnp.einsum('gts,bcsgd->bctgd', w, gv) + b_s[:, :L].T[None, None, :, :, None]
    return u * mixed


def mix_out(sb, mlp, g_sb_out, g_mlp_out, w_out):
    B, S = sb.shape[:2]
    o = jnp.concatenate([rms_norm(sb.reshape(B, S, D_SB), g_sb_out),
                         rms_norm(mlp.reshape(B, S, D_MLP), g_mlp_out)], axis=-1)
    return jnp.einsum('bse,ed->bsd', o, w_out)


def moe_ffn(h, w_router, b_router, w_gate_up, b_gate_up, w_down, b_down):
    B, S, D = h.shape
    T = B * S
    t = h.reshape(T, D)
    logits = (t @ w_router + b_router).astype(jnp.float32)
    top_logit, top_e = lax.top_k(logits, TOP_K)
    probs = jax.nn.softmax(top_logit, axis=-1).astype(h.dtype)
    n_assign = T * TOP_K
    flat_e = top_e.reshape(-1).astype(jnp.int32)
    order = jnp.argsort(flat_e)
    e_sorted = flat_e[order]
    sizes = jnp.bincount(flat_e, length=N_EXPERTS).astype(jnp.int32)
    padded = ((sizes + EXPERT_BLOCK - 1) // EXPERT_BLOCK) * EXPERT_BLOCK
    pad_end = jnp.cumsum(padded)
    pad_start = pad_end - padded
    seg_start = jnp.cumsum(sizes) - sizes
    slot = pad_start[e_sorted] + (jnp.arange(n_assign, dtype=jnp.int32) - seg_start[e_sorted])
    n_blocks = -(-n_assign // EXPERT_BLOCK) + N_EXPERTS
    n_slots = n_blocks * EXPERT_BLOCK
    slot_tok = jnp.full((n_slots,), T, jnp.int32).at[slot].set((order // TOP_K).astype(jnp.int32))
    slot_w = jnp.zeros((n_slots,), h.dtype).at[slot].set(probs.reshape(-1)[order])
    block_start = jnp.arange(n_blocks, dtype=jnp.int32) * EXPERT_BLOCK
    block_e = jnp.minimum(jnp.searchsorted(pad_end, block_start, side='right'),
                          N_EXPERTS - 1).astype(jnp.int32)
    t_pad = jnp.concatenate([t, jnp.zeros((1, D), t.dtype)], axis=0)
    xb = t_pad[slot_tok].reshape(n_blocks, EXPERT_BLOCK, D)

    def expert_block(args):
        xblk, e = args
        gu = xblk @ w_gate_up[e] + b_gate_up[e]
        gate = jnp.minimum(gu[:, :D_FF], SWIGLU_LIMIT)
        up = jnp.clip(gu[:, D_FF:], -SWIGLU_LIMIT, SWIGLU_LIMIT)
        act = gate * jax.nn.sigmoid(SWIGLU_ALPHA * gate) * (up + 1.0)
        return act @ w_down[e] + b_down[e]

    yb = lax.map(expert_block, (xb, block_e)).reshape(n_slots, D)
    y = jax.ops.segment_sum(yb * slot_w[:, None], slot_tok, num_segments=T + 1)[:T]
    return y.reshape(B, S, D)


def setup_inputs(seed: int = 0) -> dict:
    key = jax.random.key(seed)
    ks = jax.random.split(key, 25)
    f = jnp.float32
    n_pages = PAST_LEN // PAGE_SIZE
    n_pool = (DEC_BATCH * n_pages * 5) // 4
    nrm = lambda k, shape, s: jax.random.normal(k, shape, f) * s
    page_table = jax.random.permutation(ks[4], n_pool)[:DEC_BATCH * n_pages]
    page_table = page_table.reshape(DEC_BATCH, n_pages).astype(jnp.int32)
    return {
        'x_prompt': nrm(ks[0], (BATCH, SEQ, D_MODEL), 1.0),
        'x_sample': nrm(ks[1], (DEC_BATCH, DEC_SEQ, D_MODEL), 1.0),
        'cache_k': nrm(ks[2], (DEPTH, n_pool, PAGE_SIZE, H_SB, D_HEAD), 1.0),
        'cache_v': nrm(ks[3], (DEPTH, n_pool, PAGE_SIZE, H_SB, D_HEAD), 1.0),
        'page_table': page_table,
        'c_prompt': nrm(ks[5], (BATCH, D_MODEL), 1.0),
        'c_sample': nrm(ks[6], (DEC_BATCH, D_MODEL), 1.0),
        'w_ada': nrm(ks[7], (DEPTH, D_MODEL, N_ADA * D_MODEL), 0.5 * D_MODEL ** -0.5),
        'b_ada': nrm(ks[8], (DEPTH, N_ADA * D_MODEL), 0.02),
        'g_norm_mix': 1.0 + nrm(ks[9], (DEPTH, D_MODEL), 0.02),
        'g_norm_ffn': 1.0 + nrm(ks[10], (DEPTH, D_MODEL), 0.02),
        'w_in': nrm(ks[11], (DEPTH, D_MODEL, D_IN), D_MODEL ** -0.5),
        'b_sb': SB_BIAS_INIT + nrm(ks[24], (DEPTH, H_SB), 0.1),
        'w_s': nrm(ks[12], (DEPTH, G_MLP, CHUNK, CHUNK), CHUNK ** -0.5),
        'b_s': 1.0 + nrm(ks[13], (DEPTH, G_MLP, CHUNK), 0.1),
        'g_sb_out': 1.0 + nrm(ks[14], (DEPTH, D_SB), 0.02),
        'g_mlp_out': 1.0 + nrm(ks[15], (DEPTH, D_MLP), 0.02),
        'w_out': nrm(ks[16], (DEPTH, D_MIX, D_MODEL), D_MIX ** -0.5),
        'w_router': nrm(ks[17], (DEPTH, D_MODEL, N_EXPERTS), D_MODEL ** -0.5),
        'b_router': nrm(ks[18], (DEPTH, N_EXPERTS), 0.01),
        'w_gate_up': nrm(ks[19], (DEPTH, N_EXPERTS, D_MODEL, 2 * D_FF), D_MODEL ** -0.5),
        'b_gate_up': nrm(ks[20], (DEPTH, N_EXPERTS, 2 * D_FF), 0.01),
        'w_down': nrm(ks[21], (DEPTH, N_EXPERTS, D_FF, D_MODEL), D_FF ** -0.5),
        'b_down': nrm(ks[22], (DEPTH, N_EXPERTS, D_MODEL), 0.01),
        'g_final': 1.0 + nrm(ks[23], (D_MODEL,), 0.02),
    }


def reference(x_prompt, x_sample, cache_k, cache_v, page_table, c_prompt, c_sample,
              w_ada, b_ada, g_norm_mix, g_norm_ffn, w_in, b_sb, w_s, b_s, g_sb_out, g_mlp_out,
              w_out, w_router, b_router, w_gate_up, b_gate_up, w_down, b_down, g_final):
    xp, xs = x_prompt, x_sample
    B, S = xp.shape[:2]
    Bd, T = xs.shape[:2]
    kp_rows, vp_rows, ks_rows, vs_rows, gv_rows = [], [], [], [], []
    for l in range(DEPTH):
        sh1_p, sc1_p, gt1_p, sh2_p, sc2_p, gt2_p = ada_terms(c_prompt, w_ada[l], b_ada[l])
        sh1_s, sc1_s, gt1_s, sh2_s, sc2_s, gt2_s = ada_terms(c_sample, w_ada[l], b_ada[l])

        q, k, v, u, gv = project(modulate(xp, g_norm_mix[l], sh1_p, sc1_p), w_in[l])
        sb = sb_prompt(q, k, v, b_sb[l])
        mlp = spatial_gate(u.reshape(B, S // CHUNK, CHUNK, G_MLP, D_HEAD),
                           gv.reshape(B, S // CHUNK, CHUNK, G_MLP, D_HEAD), w_s[l], b_s[l])
        xp = xp + gt1_p * mix_out(sb, mlp.reshape(B, S, G_MLP, D_HEAD),
                                  g_sb_out[l], g_mlp_out[l], w_out[l])
        kp_rows.append(k.reshape(B, S // PAGE_SIZE, PAGE_SIZE, H_SB, D_HEAD))
        vp_rows.append(v.reshape(B, S // PAGE_SIZE, PAGE_SIZE, H_SB, D_HEAD))

        q, k, v, u, gv = project(modulate(xs, g_norm_mix[l], sh1_s, sc1_s), w_in[l])
        sb = sb_sample(q, k, v, b_sb[l], cache_k[l], cache_v[l], page_table)
        mlp = spatial_gate(u[:, None], gv[:, None], w_s[l], b_s[l])[:, 0]
        xs = xs + gt1_s * mix_out(sb, mlp, g_sb_out[l], g_mlp_out[l], w_out[l])
        ks_rows.append(k)
        vs_rows.append(v)
        gv_rows.append(gv.reshape(Bd, T, D_MLP))

        xp = xp + gt2_p * moe_ffn(modulate(xp, g_norm_ffn[l], sh2_p, sc2_p), w_router[l],
                                  b_router[l], w_gate_up[l], b_gate_up[l], w_down[l], b_down[l])
        xs = xs + gt2_s * moe_ffn(modulate(xs, g_norm_ffn[l], sh2_s, sc2_s), w_router[l],
                                  b_router[l], w_gate_up[l], b_gate_up[l], w_down[l], b_down[l])

    y_prompt = rms_norm(xp, g_final)
    y_sample = rms_norm(xs, g_final)
    return (y_prompt, y_sample, jnp.stack(kp_rows), jnp.stack(vp_rows),
            jnp.stack(ks_rows), jnp.stack(vs_rows), jnp.stack(gv_rows))
```

```python
import functools
import math

import jax
import jax.numpy as jnp
from jax import lax
from jax.experimental import pallas as pl
from jax.experimental.pallas import tpu as pltpu

F32 = jnp.float32
BF16 = jnp.bfloat16
I32 = jnp.int32

D_HEAD = 64
TOP_K = 4
N_ADA = 6
EPS = 1e-6
SWIGLU_LIMIT = 7.0
SWIGLU_ALPHA = 1.702
LANES = 128
SUBLANES = 8
ROW_TILE = SUBLANES
VMEM_LIMIT = 56 * 1024 * 1024

TS_MIX = 256
TR_ROUTE = 512
TD_DISPATCH = 512
BM_EXPERT = 256
TC_COMBINE = 256
PAGES_PER_STEP = 8


def _cparams(n_axes):
    return pltpu.CompilerParams(dimension_semantics=("arbitrary",) * n_axes, vmem_limit_bytes=VMEM_LIMIT)


def _rms(x):
    return x * lax.rsqrt(jnp.mean(x * x, axis=-1, keepdims=True) + EPS)


def _gelu(x):
    c = math.sqrt(2.0 / math.pi)
    return x * (0.5 * (1.0 + jnp.tanh(c * (x + 0.044715 * (x * x * x)))))


def _log_sigmoid_pair(z):
    t = jnp.log1p(jnp.exp(-jnp.abs(z)))
    lsig = jnp.minimum(z, 0.0) - t
    return lsig, lsig - z


def _sb_weights(z, crun, triu, mask):
    lsig, ls = _log_sigmoid_pair(z)
    if mask is not None:
        ls = jnp.where(mask, ls, 0.0)
    after = jnp.dot(ls.astype(BF16), triu, preferred_element_type=F32)
    cr = jnp.concatenate([crun] * (z.shape[1] // LANES), axis=1) if z.shape[1] > LANES else crun
    a = jnp.exp(lsig + after + cr)
    if mask is not None:
        a = jnp.where(mask, a, 0.0)
    return a, crun + jnp.sum(ls, axis=1, keepdims=True)


def _ada_kernel(c_ref, w_ref, b_ref, o_ref):
    c = c_ref[...]
    s = c * (1.0 / (1.0 + jnp.exp(-c)))
    o_ref[...] = jnp.dot(s.astype(BF16), w_ref[...], preferred_element_type=F32) + b_ref[...]


def _ada(c_all, w_ada_bf, b_ada):
    r, d = c_all.shape
    n = w_ada_bf.shape[1]
    return pl.pallas_call(
        _ada_kernel,
        grid=(n // d,),
        in_specs=[pl.BlockSpec((r, d), lambda j: (0, 0)),
                  pl.BlockSpec((d, d), lambda j: (0, j)),
                  pl.BlockSpec((1, d), lambda j: (0, j))],
        out_specs=pl.BlockSpec((r, d), lambda j: (0, j)),
        out_shape=jax.ShapeDtypeStruct((r, n), F32),
        compiler_params=_cparams(1),
        name="ada",
    )(c_all, w_ada_bf, b_ada.reshape(1, n))


def _mix_tail(x, sb, mlp, gt1, sh2, sc2, gsb, gml, wout, g2, wrt, brt, xm_ref, h2_ref, lg_ref):
    rows, d = x.shape
    o = jnp.concatenate([_rms(sb) * gsb, _rms(mlp) * gml], axis=1).astype(BF16)
    xm = x + gt1 * jnp.dot(o, wout, preferred_element_type=F32)
    xm_ref[...] = xm.reshape(xm_ref.shape)
    h2 = _rms(xm) * g2 * (1.0 + sc2) + sh2
    for c in range(d // LANES):
        h2_ref[pl.ds(c, rows, stride=d // LANES), :] = h2[:, c * LANES:(c + 1) * LANES]
    lg = lax.dot_general(wrt, h2.astype(BF16), (((1,), (1,)), ((), ())), preferred_element_type=F32)
    lg = lg + jnp.concatenate([brt] * (rows // LANES), axis=1)
    lg_ref[...] = lg.reshape(lg_ref.shape)


def _mix_prompt_kernel(bsb_ref, x_ref, ada_ref, g1_ref, win_ref, ws_ref, bsf_ref, triu_ref, gsb_ref, gml_ref,
                       wout_ref, g2_ref, wrt_ref, brt_ref,
                       k_ref, v_ref, xm_ref, h2_ref, lg_ref,
                       kt_buf, v_buf, o_acc, c_acc, *, ts, n_pairs, d_sb, d_mlp):
    i = pl.program_id(1)
    x = x_ref[0]
    ada = ada_ref[0]
    sh1, sc1, gt1, sh2, sc2 = ada[0:1], ada[1:2], ada[2:3], ada[3:4], ada[4:5]
    h = _rms(x) * g1_ref[...] * (1.0 + sc1) + sh1
    p = jnp.dot(h.astype(BF16), win_ref[...], preferred_element_type=F32)
    q = p[:, :d_sb] * (D_HEAD ** -0.5)
    k = p[:, d_sb:2 * d_sb]
    v = p[:, 2 * d_sb:3 * d_sb]
    u = _gelu(p[:, 3 * d_sb:3 * d_sb + d_mlp])
    gv = _gelu(p[:, 3 * d_sb + d_mlp:])
    k_ref[0] = k
    v_ref[0] = v
    kt_buf[i] = jnp.transpose(k).astype(BF16)
    v_buf[i] = v.astype(BF16)

    lane = lax.broadcasted_iota(I32, (ts, LANES), 1)
    lo = lane < D_HEAD
    row = lax.broadcasted_iota(I32, (2 * ts, ts), 0)
    col = lax.broadcasted_iota(I32, (2 * ts, ts), 1)
    diag_mask = col < jnp.where(row >= ts, row - ts, row)
    triu = triu_ref[...]
    qs = []
    for j in range(n_pairs):
        q2 = q[:, j * LANES:(j + 1) * LANES]
        qs.append(jnp.concatenate([jnp.where(lo, q2, 0.0), jnp.where(lo, 0.0, q2)], axis=0).astype(BF16))

    def pair_unit(j, c, crun, mask):
        kt = kt_buf[c, j * LANES:(j + 1) * LANES, :]
        vv = v_buf[c, :, j * LANES:(j + 1) * LANES]
        z = jnp.dot(qs[j], kt, preferred_element_type=F32)
        z = jnp.concatenate([z[:ts] + bsb_ref[2 * j], z[ts:] + bsb_ref[2 * j + 1]], axis=0)
        a, crun = _sb_weights(z, crun, triu, mask)
        return jnp.dot(a.astype(BF16), vv, preferred_element_type=F32), crun

    for j in range(n_pairs):
        o, crun = pair_unit(j, i, jnp.zeros((2 * ts, LANES), F32), diag_mask)
        o_acc[j] = o
        c_acc[j] = crun

    def chunk_body(it, carry):
        c = i - 1 - it
        for j in range(n_pairs):
            o, crun = pair_unit(j, c, c_acc[j], None)
            o_acc[j] = o_acc[j] + o
            c_acc[j] = crun
        return carry

    lax.fori_loop(0, i, chunk_body, 0)
    sb = jnp.concatenate([jnp.where(lo, o_acc[j, :ts], o_acc[j, ts:]) for j in range(n_pairs)], axis=1)

    chunk = ws_ref.shape[1]
    tr = lax.broadcasted_iota(I32, (chunk, chunk), 0)
    tc = lax.broadcasted_iota(I32, (chunk, chunk), 1)
    tril = tc <= tr
    lo_c = lax.broadcasted_iota(I32, (chunk, LANES), 1) < D_HEAD
    ws = [jnp.where(tril, ws_ref[g], 0.0).astype(BF16) for g in range(2 * n_pairs)]
    mlp_rows = []
    for r in range(ts // chunk):
        cols = []
        for j in range(n_pairs):
            gv2 = gv[r * chunk:(r + 1) * chunk, j * LANES:(j + 1) * LANES].astype(BF16)
            ma = jnp.dot(ws[2 * j], gv2, preferred_element_type=F32)
            mb = jnp.dot(ws[2 * j + 1], gv2, preferred_element_type=F32)
            mixed = jnp.where(lo_c, ma, mb) + bsf_ref[:, j * LANES:(j + 1) * LANES]
            cols.append(u[r * chunk:(r + 1) * chunk, j * LANES:(j + 1) * LANES] * mixed)
        mlp_rows.append(jnp.concatenate(cols, axis=1))
    mlp = jnp.concatenate(mlp_rows, axis=0)

    _mix_tail(x, sb, mlp, gt1, sh2, sc2, gsb_ref[...], gml_ref[...], wout_ref[...], g2_ref[...],
              wrt_ref[...], brt_ref[...], xm_ref, h2_ref, lg_ref)


def _mix_prompt(x, ada_p, b_sb, g1, win_bf, w_s, bs_full, gsb, gml, wout_bf, g2, wrt_bf, brt_b):
    b, s, d = x.shape
    ts = min(TS_MIX, s)
    n_i = s // ts
    d_sb = gsb.shape[1]
    d_mlp = gml.shape[1]
    n_pairs = d_sb // LANES
    e = wrt_bf.shape[0]
    jj = lax.broadcasted_iota(I32, (ts, ts), 0)
    ss = lax.broadcasted_iota(I32, (ts, ts), 1)
    triu = (jj > ss).astype(BF16)
    const = lambda shape: pl.BlockSpec(shape, lambda bb, ii, *_: (0,) * len(shape))
    grid_spec = pltpu.PrefetchScalarGridSpec(
        num_scalar_prefetch=1,
        grid=(b, n_i),
        in_specs=[pl.BlockSpec((1, ts, d), lambda bb, ii, *_: (bb, ii, 0)),
                  pl.BlockSpec((1, N_ADA, d), lambda bb, ii, *_: (bb, 0, 0)),
                  const((1, d)), const(win_bf.shape), const(w_s.shape), const(bs_full.shape), const((ts, ts)),
                  const((1, d_sb)), const((1, d_mlp)), const(wout_bf.shape), const((1, d)),
                  const(wrt_bf.shape), const(brt_b.shape)],
        out_specs=[pl.BlockSpec((1, ts, d_sb), lambda bb, ii, *_: (bb, ii, 0)),
                   pl.BlockSpec((1, ts, d_sb), lambda bb, ii, *_: (bb, ii, 0)),
                   pl.BlockSpec((1, ts, d), lambda bb, ii, *_: (bb, ii, 0)),
                   pl.BlockSpec((ts * (d // LANES), LANES), lambda bb, ii, *_: (bb * n_i + ii, 0)),
                   pl.BlockSpec((1, e, ts), lambda bb, ii, *_: (bb, 0, ii))],
        scratch_shapes=[pltpu.VMEM((n_i, d_sb, ts), BF16), pltpu.VMEM((n_i, ts, d_sb), BF16),
                        pltpu.VMEM((n_pairs, 2 * ts, LANES), F32), pltpu.VMEM((n_pairs, 2 * ts, LANES), F32)])
    return pl.pallas_call(
        functools.partial(_mix_prompt_kernel, ts=ts, n_pairs=n_pairs, d_sb=d_sb, d_mlp=d_mlp),
        grid_spec=grid_spec,
        out_shape=[jax.ShapeDtypeStruct((b, s, d_sb), F32), jax.ShapeDtypeStruct((b, s, d_sb), F32),
                   jax.ShapeDtypeStruct((b, s, d), F32),
                   jax.ShapeDtypeStruct((b * s * (d // LANES), LANES), F32),
                   jax.ShapeDtypeStruct((b, e, s), F32)],
        compiler_params=_cparams(2),
        name="mix_prompt",
    )(b_sb, x, ada_p, g1, win_bf, w_s, bs_full, triu, gsb, gml, wout_bf, g2, wrt_bf, brt_b)


def _s_pre_kernel(x_ref, sh_ref, sc_ref, g1_ref, win_ref, q_ref, k_ref, v_ref, u_ref, gv_ref, *, d_sb, d_mlp):
    h = _rms(x_ref[...]) * g1_ref[...] * (1.0 + sc_ref[...]) + sh_ref[...]
    p = jnp.dot(h.astype(BF16), win_ref[...], preferred_element_type=F32)
    q_ref[...] = p[:, :d_sb] * (D_HEAD ** -0.5)
    k_ref[...] = p[:, d_sb:2 * d_sb]
    v_ref[...] = p[:, 2 * d_sb:3 * d_sb]
    u_ref[...] = _gelu(p[:, 3 * d_sb:3 * d_sb + d_mlp])
    gv_ref[...] = _gelu(p[:, 3 * d_sb + d_mlp:])


def _s_pre(x2, sh1, sc1, g1, win_bf, d_sb, d_mlp):
    n, d = x2.shape
    full = lambda shape: pl.BlockSpec(shape, lambda i: (0,) * len(shape))
    return pl.pallas_call(
        functools.partial(_s_pre_kernel, d_sb=d_sb, d_mlp=d_mlp),
        grid=(1,),
        in_specs=[full((n, d)), full((n, d)), full((n, d)), full((1, d)), full(win_bf.shape)],
        out_specs=[full((n, d_sb))] * 3 + [full((n, d_mlp))] * 2,
        out_shape=[jax.ShapeDtypeStruct((n, d_sb), F32)] * 3 + [jax.ShapeDtypeStruct((n, d_mlp), F32)] * 2,
        compiler_params=_cparams(1),
        name="s_pre",
    )(x2, sh1, sc1, g1, win_bf)


def _s_attn_kernel(pt_ref, q_ref, kn_ref, vn_ref, bias_ref, triu_ref, *refs, n_pg, t_new, n_heads, page):
    k_refs = refs[:n_pg]
    v_refs = refs[n_pg:2 * n_pg]
    o_ref = refs[2 * n_pg]
    o_acc, c_acc = refs[2 * n_pg + 1:]
    jj = pl.program_id(1)
    rows = n_heads * t_new
    d_sb = q_ref.shape[2]
    q_rep = jnp.concatenate([q_ref[0]] * n_heads, axis=0)
    rr = lax.broadcasted_iota(I32, (rows, d_sb), 0)
    ll = lax.broadcasted_iota(I32, (rows, d_sb), 1)
    head_mask = (ll // D_HEAD) == (rr // t_new)
    qbd = jnp.where(head_mask, q_rep, 0.0).astype(BF16)
    bias = bias_ref[...]
    triu = triu_ref[...]

    def unit(kb, vb, crun, mask):
        z = lax.dot_general(qbd, kb, (((1,), (1,)), ((), ())), preferred_element_type=F32) + bias
        a, crun = _sb_weights(z, crun, triu, mask)
        return jnp.dot(a.astype(BF16), vb, preferred_element_type=F32), crun

    @pl.when(jj == 0)
    def _():
        pad = jnp.zeros((page - t_new, d_sb), F32)
        kb = jnp.concatenate([kn_ref[0], pad], axis=0).astype(BF16)
        vb = jnp.concatenate([vn_ref[0], pad], axis=0).astype(BF16)
        r2 = lax.broadcasted_iota(I32, (rows, page), 0)
        c2 = lax.broadcasted_iota(I32, (rows, page), 1)
        o, crun = unit(kb, vb, jnp.zeros((rows, LANES), F32), c2 < (r2 % t_new))
        o_acc[...] = o
        c_acc[...] = crun

    o = o_acc[...]
    crun = c_acc[...]
    for pg in range(n_pg):
        od, crun = unit(k_refs[pg][0].astype(BF16), v_refs[pg][0].astype(BF16), crun, None)
        o = o + od
    o_acc[...] = o
    c_acc[...] = crun

    @pl.when(jj == pl.num_programs(1) - 1)
    def _():
        om = jnp.where(head_mask, o, 0.0)
        acc = om[0:t_new]
        for hh in range(1, n_heads):
            acc = acc + om[hh * t_new:(hh + 1) * t_new]
        o_ref[0] = acc


def _s_attn(q3, kn3, vn3, bias_rows, cache_k3, cache_v3, page_table):
    bd, t_new, d_sb = q3.shape
    n_pages = page_table.shape[1]
    page = cache_k3.shape[1]
    n_heads = d_sb // D_HEAD
    n_pg = min(PAGES_PER_STEP, n_pages)
    n_j = n_pages // n_pg
    rows = n_heads * t_new
    jj = lax.broadcasted_iota(I32, (page, page), 0)
    ss = lax.broadcasted_iota(I32, (page, page), 1)
    triu = (jj > ss).astype(BF16)

    def page_spec(pg):
        return pl.BlockSpec((1, page, d_sb), lambda b, j, pt: (pt[b, n_pages - 1 - (j * n_pg + pg)], 0, 0))

    tok = pl.BlockSpec((1, t_new, d_sb), lambda b, j, pt: (b, 0, 0))
    grid_spec = pltpu.PrefetchScalarGridSpec(
        num_scalar_prefetch=1,
        grid=(bd, n_j),
        in_specs=[tok, tok, tok,
                  pl.BlockSpec((rows, LANES), lambda b, j, pt: (0, 0)),
                  pl.BlockSpec((page, page), lambda b, j, pt: (0, 0))]
                 + [page_spec(pg) for pg in range(n_pg)] * 2,
        out_specs=tok,
        scratch_shapes=[pltpu.VMEM((rows, d_sb), F32), pltpu.VMEM((rows, LANES), F32)])
    return pl.pallas_call(
        functools.partial(_s_attn_kernel, n_pg=n_pg, t_new=t_new, n_heads=n_heads, page=page),
        grid_spec=grid_spec,
        out_shape=jax.ShapeDtypeStruct((bd, t_new, d_sb), F32),
        compiler_params=_cparams(2),
        name="s_attn",
    )(page_table, q3, kn3, vn3, bias_rows, triu, *([cache_k3] * n_pg), *([cache_v3] * n_pg))


def _s_post_kernel(x_ref, sb_ref, u_ref, gv_ref, coef_ref, bt_ref, gt1_ref, sh2_ref, sc2_ref, gsb_ref, gml_ref,
                   wout_ref, g2_ref, wrt_ref, brt_ref, xm_ref, h2_ref, lg_ref, *, t_new):
    gv = gv_ref[...]
    mixed = bt_ref[...] + coef_ref[0] * gv
    for dd in range(1, t_new):
        mixed = mixed + coef_ref[dd] * pltpu.roll(gv, dd, 0)
    mlp = u_ref[...] * mixed
    _mix_tail(x_ref[...], sb_ref[...], mlp, gt1_ref[...], sh2_ref[...], sc2_ref[...], gsb_ref[...], gml_ref[...],
              wout_ref[...], g2_ref[...], wrt_ref[...], brt_ref[...], xm_ref, h2_ref, lg_ref)


def _s_post(x2, sb2, u2, gv2, coef, bias_t, gt1, sh2, sc2, gsb, gml, wout_bf, g2, wrt_bf, brt_b, t_new):
    n, d = x2.shape
    e = wrt_bf.shape[0]
    args = (x2, sb2, u2, gv2, coef, bias_t, gt1, sh2, sc2, gsb, gml, wout_bf, g2, wrt_bf, brt_b)
    full = lambda shape: pl.BlockSpec(shape, lambda i: (0,) * len(shape))
    return pl.pallas_call(
        functools.partial(_s_post_kernel, t_new=t_new),
        grid=(1,),
        in_specs=[full(a.shape) for a in args],
        out_specs=[full((n, d)), full((n * (d // LANES), LANES)), full((1, e, n))],
        out_shape=[jax.ShapeDtypeStruct((n, d), F32), jax.ShapeDtypeStruct((n * (d // LANES), LANES), F32),
                   jax.ShapeDtypeStruct((1, e, n), F32)],
        compiler_params=_cparams(1),
        name="s_post",
    )(*args)


def _route_kernel(lg_ref, tri_ref, e_ref, p_ref, r_ref, cnt_ref, base_ref):
    first = jnp.logical_and(pl.program_id(0) == 0, pl.program_id(1) == 0)

    @pl.when(first)
    def _():
        base_ref[...] = jnp.zeros_like(base_ref)

    lg = lg_ref[0]
    n_e, tr = lg.shape
    eidx = lax.broadcasted_iota(I32, (n_e, tr), 0)
    work = lg
    cnt = jnp.zeros((n_e, tr), F32)
    sels, vals = [], []
    for kk in range(TOP_K):
        m = jnp.max(work, axis=0, keepdims=True)
        ek = jnp.min(jnp.where(work == m, eidx, n_e), axis=0, keepdims=True)
        sel = eidx == ek
        work = jnp.where(sel, -jnp.inf, work)
        cnt = cnt + sel.astype(F32)
        e_ref[0, kk:kk + 1, :] = ek
        sels.append(sel)
        vals.append(m)
    ex = [jnp.exp(vv - vals[0]) for vv in vals]
    den = ex[0] + ex[1] + ex[2] + ex[3]
    for kk in range(TOP_K):
        p_ref[0, kk:kk + 1, :] = ex[kk] / den
    before = jnp.dot(cnt.astype(BF16), tri_ref[...], preferred_element_type=F32)
    base = base_ref[...]
    pos = before + jnp.concatenate([base] * (tr // LANES), axis=1)
    for kk in range(TOP_K):
        r_ref[0, kk:kk + 1, :] = jnp.sum(jnp.where(sels[kk], pos, 0.0), axis=0, keepdims=True).astype(I32)
    base = base + jnp.sum(cnt, axis=1, keepdims=True)
    base_ref[...] = base
    cnt_ref[...] = base.astype(I32)


def _route(lgt):
    b, n_e, s = lgt.shape
    tr = min(TR_ROUTE, s)
    t1 = lax.broadcasted_iota(I32, (tr, tr), 0)
    t2 = lax.broadcasted_iota(I32, (tr, tr), 1)
    tri = (t1 < t2).astype(BF16)
    tok = pl.BlockSpec((1, TOP_K, tr), lambda bb, ii: (bb, 0, ii))
    return pl.pallas_call(
        _route_kernel,
        grid=(b, s // tr),
        in_specs=[pl.BlockSpec((1, n_e, tr), lambda bb, ii: (bb, 0, ii)),
                  pl.BlockSpec((tr, tr), lambda bb, ii: (0, 0))],
        out_specs=[tok, tok, tok, pl.BlockSpec((n_e, LANES), lambda bb, ii: (0, 0))],
        out_shape=[jax.ShapeDtypeStruct((b, TOP_K, s), I32), jax.ShapeDtypeStruct((b, TOP_K, s), F32),
                   jax.ShapeDtypeStruct((b, TOP_K, s), I32), jax.ShapeDtypeStruct((n_e, LANES), I32)],
        scratch_shapes=[pltpu.VMEM((n_e, LANES), F32)],
        compiler_params=_cparams(2),
        name="route",
    )(lgt, tri)


def _slot_kernel(ps_ref, e_ref, r_ref, s_ref, *, n_e):
    e = e_ref[...]
    slot = r_ref[...]
    for ee in range(n_e):
        slot = slot + jnp.where(e == ee, ps_ref[ee], 0)
    s_ref[...] = slot


def _slots(pad_start, e_t, r_t, n_e):
    b, kk, s = e_t.shape
    blk = pl.BlockSpec((1, kk, s), lambda bb, ps: (bb, 0, 0))
    return pl.pallas_call(
        functools.partial(_slot_kernel, n_e=n_e),
        grid_spec=pltpu.PrefetchScalarGridSpec(num_scalar_prefetch=1, grid=(b,), in_specs=[blk, blk],
                                               out_specs=blk),
        out_shape=jax.ShapeDtypeStruct((b, kk, s), I32),
        compiler_params=_cparams(1),
        name="slots",
    )(pad_start, e_t, r_t)


def _row_copy(src_ref, src_row, dst_ref, dst_row, sem):
    return pltpu.make_async_copy(src_ref.at[pl.ds(pl.multiple_of(src_row * ROW_TILE, ROW_TILE), ROW_TILE)],
                                 dst_ref.at[pl.ds(pl.multiple_of(dst_row * ROW_TILE, ROW_TILE), ROW_TILE)], sem)


def _dispatch_kernel(ps_ref, sz_ref, pd_ref, slot_ref, h_ref, xs_ref, zero_ref, sem, zsem, *, td, n_e):
    first = jnp.logical_and(pl.program_id(0) == 0, pl.program_id(1) == 0)

    @pl.when(first)
    def _():
        zero_ref[...] = jnp.zeros_like(zero_ref)

        def per_expert(ee, carry):
            def start(r, c):
                _row_copy(zero_ref, 0, xs_ref, ps_ref[ee] + r, zsem).start()
                return c

            def wait(r, c):
                _row_copy(zero_ref, 0, xs_ref, ps_ref[ee] + r, zsem).wait()
                return c

            lax.fori_loop(sz_ref[ee], pd_ref[ee], start, 0)
            lax.fori_loop(sz_ref[ee], pd_ref[ee], wait, 0)
            return carry

        lax.fori_loop(0, n_e, per_expert, 0)

    def start(t, c):
        for kk in range(TOP_K):
            _row_copy(h_ref, t, xs_ref, slot_ref[0, kk, t], sem).start()
        return c

    def wait(t, c):
        for kk in range(TOP_K):
            _row_copy(h_ref, t, xs_ref, slot_ref[0, kk, t], sem).wait()
        return c

    lax.fori_loop(0, td, start, 0)
    lax.fori_loop(0, td, wait, 0)


def _dispatch(pad_start, sizes, padded, slot_t, h2_rows, n_slots):
    b, kk, s = slot_t.shape
    td = min(TD_DISPATCH, s)
    n_i = s // td
    n_e = sizes.shape[0]
    grid_spec = pltpu.PrefetchScalarGridSpec(
        num_scalar_prefetch=3,
        grid=(b, n_i),
        in_specs=[pl.BlockSpec((1, kk, td), lambda bb, ii, *_: (bb, 0, ii), memory_space=pltpu.SMEM),
                  pl.BlockSpec((td * ROW_TILE, LANES), lambda bb, ii, *_: (bb * n_i + ii, 0))],
        out_specs=pl.BlockSpec(memory_space=pl.ANY),
        scratch_shapes=[pltpu.VMEM((ROW_TILE, LANES), F32), pltpu.SemaphoreType.DMA(()),
                        pltpu.SemaphoreType.DMA(())])
    return pl.pallas_call(
        functools.partial(_dispatch_kernel, td=td, n_e=n_e),
        grid_spec=grid_spec,
        out_shape=jax.ShapeDtypeStruct((n_slots * ROW_TILE, LANES), F32),
        compiler_params=_cparams(2),
        name="dispatch",
    )(pad_start, sizes, padded, slot_t, h2_rows)


def _expert_kernel(be_ref, nu_ref, xs_ref, wgu_ref, bgu_ref, wd_ref, bd_ref, yb_ref, *, bm, d, d_ff):
    blk = pl.program_id(0)
    n_c = d // LANES

    @pl.when(blk < nu_ref[0])
    def _():
        x = jnp.concatenate([xs_ref[pl.ds(c, bm, stride=n_c), :] for c in range(n_c)], axis=1)
        gu = jnp.dot(x.astype(BF16), wgu_ref[0], preferred_element_type=F32) + bgu_ref[0]
        gate = jnp.minimum(gu[:, :d_ff], SWIGLU_LIMIT)
        up = jnp.clip(gu[:, d_ff:], -SWIGLU_LIMIT, SWIGLU_LIMIT)
        act = gate * (1.0 / (1.0 + jnp.exp(-SWIGLU_ALPHA * gate))) * (up + 1.0)
        y = jnp.dot(act.astype(BF16), wd_ref[0], preferred_element_type=F32) + bd_ref[0]
        for c in range(n_c):
            yb_ref[pl.ds(c, bm, stride=n_c), :] = y[:, c * LANES:(c + 1) * LANES]

    @pl.when(blk >= nu_ref[0])
    def _():
        yb_ref[...] = jnp.zeros_like(yb_ref)


def _experts(block_e, n_used, xs_rows, wgu_bf, bgu, wd_bf, bd, n_blocks):
    n_e, d, d_ff2 = wgu_bf.shape
    d_ff = d_ff2 // 2
    bm = BM_EXPERT
    grid_spec = pltpu.PrefetchScalarGridSpec(
        num_scalar_prefetch=2,
        grid=(n_blocks,),
        in_specs=[pl.BlockSpec((bm * ROW_TILE, LANES), lambda i, be, nu: (i, 0)),
                  pl.BlockSpec((1, d, d_ff2), lambda i, be, nu: (be[i], 0, 0)),
                  pl.BlockSpec((1, 1, d_ff2), lambda i, be, nu: (be[i], 0, 0)),
                  pl.BlockSpec((1, d_ff, d), lambda i, be, nu: (be[i], 0, 0)),
                  pl.BlockSpec((1, 1, d), lambda i, be, nu: (be[i], 0, 0))],
        out_specs=pl.BlockSpec((bm * ROW_TILE, LANES), lambda i, be, nu: (i, 0)))
    return pl.pallas_call(
        functools.partial(_expert_kernel, bm=bm, d=d, d_ff=d_ff),
        grid_spec=grid_spec,
        out_shape=jax.ShapeDtypeStruct((n_blocks * bm * ROW_TILE, LANES), F32),
        compiler_params=_cparams(1),
        name="experts",
    )(block_e, n_used, xs_rows, wgu_bf, bgu.reshape(n_e, 1, d_ff2), wd_bf, bd.reshape(n_e, 1, d))


def _combine_kernel(slot_ref, p_ref, xm_ref, gt_ref, gf_ref, yb_ref, y_ref, gbuf, sem, *, tc, d):
    n_c = d // LANES

    def start(t, c):
        for kk in range(TOP_K):
            _row_copy(yb_ref, slot_ref[0, kk, t], gbuf, kk * tc + t, sem).start()
        return c

    def wait(t, c):
        for kk in range(TOP_K):
            _row_copy(yb_ref, slot_ref[0, kk, t], gbuf, kk * tc + t, sem).wait()
        return c

    lax.fori_loop(0, tc, start, 0)
    lax.fori_loop(0, tc, wait, 0)
    p = p_ref[0]
    xm = xm_ref[0]
    gt2 = gt_ref[0]
    cols = []
    for c in range(n_c):
        acc = p[:, 0:1] * gbuf[pl.ds(c, tc, stride=n_c), :]
        for kk in range(1, TOP_K):
            acc = acc + p[:, kk:kk + 1] * gbuf[pl.ds(kk * tc * n_c + c, tc, stride=n_c), :]
        cols.append(acc)
    y = jnp.concatenate(cols, axis=1)
    xo = xm + gt2 * y
    y_ref[0] = _rms(xo) * gf_ref[...]


def _combine(slot_t, p_tok, xm, gt2, gfin, yb_rows):
    b, s, d = xm.shape
    tc = min(TC_COMBINE, s)
    n_i = s // tc
    kk = slot_t.shape[1]
    if gt2.shape[1] == 1:
        gt_spec = pl.BlockSpec((1, 1, d), lambda bb, ii: (bb, 0, 0))
    else:
        gt_spec = pl.BlockSpec((1, tc, d), lambda bb, ii: (bb, ii, 0))
    return pl.pallas_call(
        functools.partial(_combine_kernel, tc=tc, d=d),
        grid=(b, n_i),
        in_specs=[pl.BlockSpec((1, kk, tc), lambda bb, ii: (bb, 0, ii), memory_space=pltpu.SMEM),
                  pl.BlockSpec((1, tc, kk), lambda bb, ii: (bb, ii, 0)),
                  pl.BlockSpec((1, tc, d), lambda bb, ii: (bb, ii, 0)),
                  gt_spec,
                  pl.BlockSpec((1, d), lambda bb, ii: (0, 0)),
                  pl.BlockSpec(memory_space=pl.ANY)],
        out_specs=pl.BlockSpec((1, tc, d), lambda bb, ii: (bb, ii, 0)),
        out_shape=jax.ShapeDtypeStruct((b, s, d), F32),
        scratch_shapes=[pltpu.VMEM((TOP_K * tc * ROW_TILE, LANES), F32), pltpu.SemaphoreType.DMA(())],
        compiler_params=_cparams(2),
        name="combine",
    )(slot_t, p_tok, xm, gt2, gfin, yb_rows)


def _moe(lgt, h2_rows, xm, gt2, gfin, wgu_bf, bgu, wd_bf, bd):
    b, n_e, s = lgt.shape
    bm = BM_EXPERT
    n_assign = b * s * TOP_K
    n_blocks = -(-n_assign // bm) + n_e
    n_slots = n_blocks * bm
    e_t, p_t, r_t, cnt = _route(lgt)
    sizes = cnt[:, 0]
    padded = ((sizes + bm - 1) // bm) * bm
    pad_end = jnp.cumsum(padded)
    pad_start = pad_end - padded
    block_e = jnp.minimum(jnp.searchsorted(pad_end, jnp.arange(n_blocks, dtype=I32) * bm, side="right"),
                          n_e - 1).astype(I32)
    n_used = (pad_end[-1:] // bm).astype(I32)
    slot_t = _slots(pad_start, e_t, r_t, n_e)
    xs_rows = _dispatch(pad_start, sizes, padded, slot_t, h2_rows, n_slots)
    yb_rows = _experts(block_e, n_used, xs_rows, wgu_bf, bgu, wd_bf, bd, n_blocks)
    return _combine(slot_t, jnp.transpose(p_t, (0, 2, 1)), xm, gt2, gfin, yb_rows)


def kernel(x_prompt, x_sample, cache_k, cache_v, page_table, c_prompt, c_sample, w_ada, b_ada, g_norm_mix,
           g_norm_ffn, w_in, b_sb, w_s, b_s, g_sb_out, g_mlp_out, w_out, w_router, b_router, w_gate_up,
           b_gate_up, w_down, b_down, g_final):
    b, s, d = x_prompt.shape
    bd, t_new, _ = x_sample.shape
    depth, n_pool, page, h_sb, d_head = cache_k.shape
    assert depth == 1 and d_head == D_HEAD
    d_sb = h_sb * d_head
    d_mlp = g_mlp_out.shape[1]
    g_mlp = d_mlp // d_head
    n_e = w_router.shape[2]
    chunk = w_s.shape[2]
    n_tok_s = bd * t_new

    win_bf = w_in[0].astype(BF16)
    wout_bf = w_out[0].astype(BF16)
    wada_bf = w_ada[0].astype(BF16)
    wrt_bf = jnp.transpose(w_router[0]).astype(BF16)
    wgu_bf = w_gate_up[0].astype(BF16)
    wd_bf = w_down[0].astype(BF16)
    brt_b = jnp.broadcast_to(b_router[0][:, None], (n_e, LANES))
    g1 = g_norm_mix[0].reshape(1, d)
    g2 = g_norm_ffn[0].reshape(1, d)
    gsb = g_sb_out[0].reshape(1, d_sb)
    gml = g_mlp_out[0].reshape(1, d_mlp)
    gfin = g_final.reshape(1, d)
    bs_full = jnp.repeat(jnp.transpose(b_s[0]), d_head, axis=1)
    bias_rows = jnp.broadcast_to(jnp.repeat(b_sb[0], t_new)[:, None], (h_sb * t_new, LANES))

    ada = _ada(jnp.concatenate([c_prompt, c_sample], axis=0), wada_bf, b_ada[0])
    ada_p = ada[:b].reshape(b, N_ADA, d)
    ada_s = jnp.repeat(ada[b:].reshape(bd, N_ADA, d), t_new, axis=0)

    k_p, v_p, xm_p, h2_p, lg_p = _mix_prompt(x_prompt, ada_p, b_sb[0], g1, win_bf, w_s[0], bs_full, gsb, gml,
                                             wout_bf, g2, wrt_bf, brt_b)
    y_prompt = _moe(lg_p, h2_p, xm_p, ada_p[:, 5:6, :], gfin, wgu_bf, b_gate_up[0], wd_bf, b_down[0])

    x_s2 = x_sample.reshape(n_tok_s, d)
    q_s, k_s, v_s, u_s, gv_s = _s_pre(x_s2, ada_s[:, 0], ada_s[:, 1], g1, win_bf, d_sb, d_mlp)
    sb_s = _s_attn(q_s.reshape(bd, t_new, d_sb), k_s.reshape(bd, t_new, d_sb), v_s.reshape(bd, t_new, d_sb),
                   bias_rows, cache_k[0].reshape(n_pool, page, d_sb), cache_v[0].reshape(n_pool, page, d_sb),
                   page_table)
    w_tril = jnp.tril(w_s[0][:, :t_new, :t_new])
    tt = jnp.arange(t_new)
    coef = jnp.stack([jnp.where((tt >= dd)[None, :], w_tril[:, tt, jnp.maximum(tt - dd, 0)], 0.0)
                      for dd in range(t_new)])
    coef = jnp.tile(jnp.repeat(jnp.transpose(coef, (0, 2, 1)), d_head, axis=2), (1, bd, 1))
    bias_t = jnp.tile(jnp.repeat(jnp.transpose(b_s[0][:, :t_new]), d_head, axis=1), (bd, 1))
    xm_s, h2_s, lg_s = _s_post(x_s2, sb_s.reshape(n_tok_s, d_sb), u_s, gv_s, coef, bias_t, ada_s[:, 2],
                               ada_s[:, 3], ada_s[:, 4], gsb, gml, wout_bf, g2, wrt_bf, brt_b, t_new)
    y_sample = _moe(lg_s, h2_s, xm_s.reshape(1, n_tok_s, d), ada_s[:, 5].reshape(1, n_tok_s, d), gfin,
                    wgu_bf, b_gate_up[0], wd_bf, b_down[0])

    return (y_prompt, y_sample.reshape(bd, t_new, d),
            k_p.reshape(1, b, s // page, page, h_sb, d_head), v_p.reshape(1, b, s // page, page, h_sb, d_head),
            k_s.reshape(1, bd, t_new, h_sb, d_head), v_s.reshape(1, bd, t_new, h_sb, d_head),
            gv_s.reshape(1, bd, t_new, d_mlp))
```

```python
import functools
import math

import jax
import jax.numpy as jnp
from jax import lax
from jax.experimental import pallas as pl
from jax.experimental.pallas import tpu as pltpu

F32 = jnp.float32
BF16 = jnp.bfloat16
I32 = jnp.int32

D_HEAD = 64
TOP_K = 4
N_ADA = 6
EPS = 1e-6
SWIGLU_LIMIT = 7.0
SWIGLU_ALPHA = 1.702
LANES = 128
SUBLANES = 8
ROW_TILE = SUBLANES
VMEM_LIMIT = 56 * 1024 * 1024

TS_MIX = 256
TR_ROUTE = 512
TD_DISPATCH = 512
BM_EXPERT = 512
BM_EXPERT_SMALL = 128
TC_COMBINE = 256
PAGES_PER_STEP = 8


def _cparams(n_axes):
    return pltpu.CompilerParams(dimension_semantics=("arbitrary",) * n_axes, vmem_limit_bytes=VMEM_LIMIT)


def _rms(x):
    return x * lax.rsqrt(jnp.mean(x * x, axis=-1, keepdims=True) + EPS)


def _gelu(x):
    c = math.sqrt(2.0 / math.pi)
    return x * (0.5 * (1.0 + jnp.tanh(c * (x + 0.044715 * (x * x * x)))))


def _log_sigmoid_pair(z):
    t = jnp.log(1.0 + jnp.exp(-jnp.abs(z)))
    lsig = jnp.minimum(z, 0.0) - t
    return lsig, lsig - z


def _sb_weights(z, crun, triu, mask):
    lsig, ls = _log_sigmoid_pair(z)
    if mask is not None:
        ls = jnp.where(mask, ls, 0.0)
    after = jnp.dot(ls.astype(BF16), triu, preferred_element_type=F32)
    cr = jnp.concatenate([crun] * (z.shape[1] // LANES), axis=1) if z.shape[1] > LANES else crun
    a = jnp.exp(lsig + after + cr)
    if mask is not None:
        a = jnp.where(mask, a, 0.0)
    return a, crun + jnp.sum(ls, axis=1, keepdims=True)


def _ada_kernel(c_ref, w_ref, b_ref, o_ref):
    c = c_ref[...]
    s = c * (1.0 / (1.0 + jnp.exp(-c)))
    o_ref[...] = jnp.dot(s.astype(BF16), w_ref[...], preferred_element_type=F32) + b_ref[...]


def _ada(c_all, w_ada_bf, b_ada):
    r, d = c_all.shape
    n = w_ada_bf.shape[1]
    return pl.pallas_call(
        _ada_kernel,
        grid=(n // d,),
        in_specs=[pl.BlockSpec((r, d), lambda j: (0, 0)),
                  pl.BlockSpec((d, d), lambda j: (0, j)),
                  pl.BlockSpec((1, d), lambda j: (0, j))],
        out_specs=pl.BlockSpec((r, d), lambda j: (0, j)),
        out_shape=jax.ShapeDtypeStruct((r, n), F32),
        compiler_params=_cparams(1),
        name="ada",
    )(c_all, w_ada_bf, b_ada.reshape(1, n))


def _mix_tail(x, sb, mlp, gt1, sh2, sc2, gsb, gml, wout, g2, wrt, brt, xm_ref, h2_ref, lg_ref):
    rows, d = x.shape
    o = jnp.concatenate([_rms(sb) * gsb, _rms(mlp) * gml], axis=1).astype(BF16)
    xm = x + gt1 * jnp.dot(o, wout, preferred_element_type=F32)
    xm_ref[...] = xm.reshape(xm_ref.shape)
    h2 = _rms(xm) * g2 * (1.0 + sc2) + sh2
    for c in range(d // LANES):
        h2_ref[pl.ds(c, rows, stride=d // LANES), :] = h2[:, c * LANES:(c + 1) * LANES]
    lg = lax.dot_general(wrt, h2.astype(BF16), (((1,), (1,)), ((), ())), preferred_element_type=F32)
    lg = lg + jnp.concatenate([brt] * (rows // LANES), axis=1)
    lg_ref[...] = lg.reshape(lg_ref.shape)


def _mix_prompt_kernel(bsb_ref, x_ref, ada_ref, g1_ref, win_ref, ws_ref, bsf_ref, triu_ref, gsb_ref, gml_ref,
                       wout_ref, g2_ref, wrt_ref, brt_ref,
                       k_ref, v_ref, xm_ref, h2_ref, lg_ref,
                       kt_buf, v_buf, o_acc, c_acc, *, ts, n_pairs, d_sb, d_mlp):
    i = pl.program_id(1)
    x = x_ref[0]
    ada = ada_ref[0]
    sh1, sc1, gt1, sh2, sc2 = ada[0:1], ada[1:2], ada[2:3], ada[3:4], ada[4:5]
    h = _rms(x) * g1_ref[...] * (1.0 + sc1) + sh1
    p = jnp.dot(h.astype(BF16), win_ref[...], preferred_element_type=F32)
    q = p[:, :d_sb] * (D_HEAD ** -0.5)
    k = p[:, d_sb:2 * d_sb]
    v = p[:, 2 * d_sb:3 * d_sb]
    u = _gelu(p[:, 3 * d_sb:3 * d_sb + d_mlp])
    gv = _gelu(p[:, 3 * d_sb + d_mlp:])
    k_ref[0] = k
    v_ref[0] = v
    kt_buf[i] = jnp.transpose(k).astype(BF16)
    v_buf[i] = v.astype(BF16)

    lane = lax.broadcasted_iota(I32, (ts, LANES), 1)
    lo = lane < D_HEAD
    row = lax.broadcasted_iota(I32, (2 * ts, ts), 0)
    col = lax.broadcasted_iota(I32, (2 * ts, ts), 1)
    diag_mask = col < jnp.where(row >= ts, row - ts, row)
    triu = triu_ref[...]
    qs = []
    for j in range(n_pairs):
        q2 = q[:, j * LANES:(j + 1) * LANES]
        qs.append(jnp.concatenate([jnp.where(lo, q2, 0.0), jnp.where(lo, 0.0, q2)], axis=0).astype(BF16))

    def pair_unit(j, c, crun, mask):
        kt = kt_buf[c, j * LANES:(j + 1) * LANES, :]
        vv = v_buf[c, :, j * LANES:(j + 1) * LANES]
        z = jnp.dot(qs[j], kt, preferred_element_type=F32)
        z = jnp.concatenate([z[:ts] + bsb_ref[2 * j], z[ts:] + bsb_ref[2 * j + 1]], axis=0)
        a, crun = _sb_weights(z, crun, triu, mask)
        return jnp.dot(a.astype(BF16), vv, preferred_element_type=F32), crun

    for j in range(n_pairs):
        o, crun = pair_unit(j, i, jnp.zeros((2 * ts, LANES), F32), diag_mask)
        o_acc[j] = o
        c_acc[j] = crun

    def chunk_body(it, carry):
        c = i - 1 - it
        for j in range(n_pairs):
            o, crun = pair_unit(j, c, c_acc[j], None)
            o_acc[j] = o_acc[j] + o
            c_acc[j] = crun
        return carry

    lax.fori_loop(0, i, chunk_body, 0)
    sb = jnp.concatenate([jnp.where(lo, o_acc[j, :ts], o_acc[j, ts:]) for j in range(n_pairs)], axis=1)

    chunk = ws_ref.shape[1]
    tr = lax.broadcasted_iota(I32, (chunk, chunk), 0)
    tc = lax.broadcasted_iota(I32, (chunk, chunk), 1)
    tril = tc <= tr
    lo_c = lax.broadcasted_iota(I32, (chunk, LANES), 1) < D_HEAD
    ws = [jnp.where(tril, ws_ref[g], 0.0).astype(BF16) for g in range(2 * n_pairs)]
    mlp_rows = []
    for r in range(ts // chunk):
        cols = []
        for j in range(n_pairs):
            gv2 = gv[r * chunk:(r + 1) * chunk, j * LANES:(j + 1) * LANES].astype(BF16)
            ma = jnp.dot(ws[2 * j], gv2, preferred_element_type=F32)
            mb = jnp.dot(ws[2 * j + 1], gv2, preferred_element_type=F32)
            mixed = jnp.where(lo_c, ma, mb) + bsf_ref[:, j * LANES:(j + 1) * LANES]
            cols.append(u[r * chunk:(r + 1) * chunk, j * LANES:(j + 1) * LANES] * mixed)
        mlp_rows.append(jnp.concatenate(cols, axis=1))
    mlp = jnp.concatenate(mlp_rows, axis=0)

    _mix_tail(x, sb, mlp, gt1, sh2, sc2, gsb_ref[...], gml_ref[...], wout_ref[...], g2_ref[...],
              wrt_ref[...], brt_ref[...], xm_ref, h2_ref, lg_ref)


def _mix_prompt(x, ada_p, b_sb, g1, win_bf, w_s, bs_full, gsb, gml, wout_bf, g2, wrt_bf, brt_b):
    b, s, d = x.shape
    ts = min(TS_MIX, s)
    n_i = s // ts
    d_sb = gsb.shape[1]
    d_mlp = gml.shape[1]
    n_pairs = d_sb // LANES
    e = wrt_bf.shape[0]
    jj = lax.broadcasted_iota(I32, (ts, ts), 0)
    ss = lax.broadcasted_iota(I32, (ts, ts), 1)
    triu = (jj > ss).astype(BF16)
    const = lambda shape: pl.BlockSpec(shape, lambda bb, ii, *_: (0,) * len(shape))
    grid_spec = pltpu.PrefetchScalarGridSpec(
        num_scalar_prefetch=1,
        grid=(b, n_i),
        in_specs=[pl.BlockSpec((1, ts, d), lambda bb, ii, *_: (bb, ii, 0)),
                  pl.BlockSpec((1, N_ADA, d), lambda bb, ii, *_: (bb, 0, 0)),
                  const((1, d)), const(win_bf.shape), const(w_s.shape), const(bs_full.shape), const((ts, ts)),
                  const((1, d_sb)), const((1, d_mlp)), const(wout_bf.shape), const((1, d)),
                  const(wrt_bf.shape), const(brt_b.shape)],
        out_specs=[pl.BlockSpec((1, ts, d_sb), lambda bb, ii, *_: (bb, ii, 0)),
                   pl.BlockSpec((1, ts, d_sb), lambda bb, ii, *_: (bb, ii, 0)),
                   pl.BlockSpec((1, ts, d), lambda bb, ii, *_: (bb, ii, 0)),
                   pl.BlockSpec((ts * (d // LANES), LANES), lambda bb, ii, *_: (bb * n_i + ii, 0)),
                   pl.BlockSpec((1, e, ts), lambda bb, ii, *_: (bb, 0, ii))],
        scratch_shapes=[pltpu.VMEM((n_i, d_sb, ts), BF16), pltpu.VMEM((n_i, ts, d_sb), BF16),
                        pltpu.VMEM((n_pairs, 2 * ts, LANES), F32), pltpu.VMEM((n_pairs, 2 * ts, LANES), F32)])
    return pl.pallas_call(
        functools.partial(_mix_prompt_kernel, ts=ts, n_pairs=n_pairs, d_sb=d_sb, d_mlp=d_mlp),
        grid_spec=grid_spec,
        out_shape=[jax.ShapeDtypeStruct((b, s, d_sb), F32), jax.ShapeDtypeStruct((b, s, d_sb), F32),
                   jax.ShapeDtypeStruct((b, s, d), F32),
                   jax.ShapeDtypeStruct((b * s * (d // LANES), LANES), F32),
                   jax.ShapeDtypeStruct((b, e, s), F32)],
        compiler_params=_cparams(2),
        name="mix_prompt",
    )(b_sb, x, ada_p, g1, win_bf, w_s, bs_full, triu, gsb, gml, wout_bf, g2, wrt_bf, brt_b)


def _s_pre_kernel(x_ref, sh_ref, sc_ref, g1_ref, win_ref, q_ref, k_ref, v_ref, u_ref, gv_ref, *, d_sb, d_mlp):
    h = _rms(x_ref[...]) * g1_ref[...] * (1.0 + sc_ref[...]) + sh_ref[...]
    p = jnp.dot(h.astype(BF16), win_ref[...], preferred_element_type=F32)
    q_ref[...] = p[:, :d_sb] * (D_HEAD ** -0.5)
    k_ref[...] = p[:, d_sb:2 * d_sb]
    v_ref[...] = p[:, 2 * d_sb:3 * d_sb]
    u_ref[...] = _gelu(p[:, 3 * d_sb:3 * d_sb + d_mlp])
    gv_ref[...] = _gelu(p[:, 3 * d_sb + d_mlp:])


def _s_pre(x2, sh1, sc1, g1, win_bf, d_sb, d_mlp):
    n, d = x2.shape
    full = lambda shape: pl.BlockSpec(shape, lambda i: (0,) * len(shape))
    return pl.pallas_call(
        functools.partial(_s_pre_kernel, d_sb=d_sb, d_mlp=d_mlp),
        grid=(1,),
        in_specs=[full((n, d)), full((n, d)), full((n, d)), full((1, d)), full(win_bf.shape)],
        out_specs=[full((n, d_sb))] * 3 + [full((n, d_mlp))] * 2,
        out_shape=[jax.ShapeDtypeStruct((n, d_sb), F32)] * 3 + [jax.ShapeDtypeStruct((n, d_mlp), F32)] * 2,
        compiler_params=_cparams(1),
        name="s_pre",
    )(x2, sh1, sc1, g1, win_bf)


def _sb_columns(z, crun, tri, mask, vmat):
    gw = tri.shape[0]
    n_g = z.shape[1] // gw
    rows = z.shape[0]
    lsig, ls = _log_sigmoid_pair(z)
    ls = jnp.where(mask, ls, 0.0)
    parts = [ls[:, g * gw:(g + 1) * gw] for g in range(n_g)]
    aft = jnp.dot(jnp.concatenate(parts, axis=0).astype(BF16), tri, preferred_element_type=F32)
    cols = [None] * n_g
    for g in reversed(range(n_g)):
        cols[g] = aft[g * rows:(g + 1) * rows] + jnp.concatenate([crun] * (gw // LANES), axis=1)
        crun = crun + jnp.sum(parts[g], axis=1, keepdims=True)
    a = jnp.where(mask, jnp.exp(lsig + jnp.concatenate(cols, axis=1)), 0.0)
    return jnp.dot(a.astype(BF16), vmat, preferred_element_type=F32), crun


def _s_attn_kernel(pt_ref, q_ref, kn_ref, vn_ref, bias_ref, tri_ref, *refs, n_pg, t_new, n_heads):
    k_refs = refs[:n_pg]
    v_refs = refs[n_pg:2 * n_pg]
    o_ref = refs[2 * n_pg]
    o_acc, c_acc = refs[2 * n_pg + 1:]
    jj = pl.program_id(1)
    rows = n_heads * t_new
    n_cols = k_refs[0].shape[1]
    gw = tri_ref.shape[0]
    q = q_ref[0].astype(BF16)
    tri = tri_ref[...]
    rr = lax.broadcasted_iota(I32, (rows, n_cols), 0)
    cc = lax.broadcasted_iota(I32, (rows, n_cols), 1)
    own_head = (cc % n_heads) == (rr // t_new)
    bias = jnp.concatenate([bias_ref[...]] * (n_cols // LANES), axis=1)

    def logits(kmat, width):
        z = lax.dot_general(q, kmat, (((1,), (1,)), ((), ())), preferred_element_type=F32)
        return z + bias[:, :width]

    @pl.when(jj == 0)
    def _():
        n_new = kn_ref.shape[1]
        pad = jnp.zeros((gw - n_new, kn_ref.shape[2]), F32)
        kb = jnp.concatenate([kn_ref[0], pad], axis=0).astype(BF16)
        vb = jnp.concatenate([vn_ref[0], pad], axis=0).astype(BF16)
        r2 = lax.broadcasted_iota(I32, (rows, gw), 0)
        c2 = lax.broadcasted_iota(I32, (rows, gw), 1)
        mask = jnp.logical_and((c2 % n_heads) == (r2 // t_new), (c2 // n_heads) < (r2 % t_new))
        o, crun = _sb_columns(logits(kb, gw), jnp.zeros((rows, LANES), F32), tri, mask, vb)
        o_acc[...] = o
        c_acc[...] = crun

    o = o_acc[...]
    crun = c_acc[...]
    for pg in range(n_pg):
        od, crun = _sb_columns(logits(k_refs[pg][0].astype(BF16), n_cols), crun, tri, own_head,
                               v_refs[pg][0].astype(BF16))
        o = o + od
    o_acc[...] = o
    c_acc[...] = crun

    @pl.when(jj == pl.num_programs(1) - 1)
    def _():
        o_ref[0] = o


def _s_attn(q_ht, kn_sh, vn_sh, bias_rows, cache_k_rows, cache_v_rows, page_table, t_new):
    bd, rows, d_head = q_ht.shape
    n_heads = rows // t_new
    n_pages = page_table.shape[1]
    n_cols = cache_k_rows.shape[1]
    n_pg = min(PAGES_PER_STEP, n_pages)
    n_j = n_pages // n_pg
    gw = 2 * LANES
    jj = lax.broadcasted_iota(I32, (gw, gw), 0)
    ss = lax.broadcasted_iota(I32, (gw, gw), 1)
    tri = (jj > ss).astype(BF16)

    def page_spec(pg):
        return pl.BlockSpec((1, n_cols, d_head), lambda b, j, pt: (pt[b, n_pages - 1 - (j * n_pg + pg)], 0, 0))

    tok = pl.BlockSpec((1, rows, d_head), lambda b, j, pt: (b, 0, 0))
    grid_spec = pltpu.PrefetchScalarGridSpec(
        num_scalar_prefetch=1,
        grid=(bd, n_j),
        in_specs=[tok, tok, tok,
                  pl.BlockSpec((rows, LANES), lambda b, j, pt: (0, 0)),
                  pl.BlockSpec((gw, gw), lambda b, j, pt: (0, 0))]
                 + [page_spec(pg) for pg in range(n_pg)] * 2,
        out_specs=tok,
        scratch_shapes=[pltpu.VMEM((rows, d_head), F32), pltpu.VMEM((rows, LANES), F32)])
    return pl.pallas_call(
        functools.partial(_s_attn_kernel, n_pg=n_pg, t_new=t_new, n_heads=n_heads),
        grid_spec=grid_spec,
        out_shape=jax.ShapeDtypeStruct((bd, rows, d_head), F32),
        compiler_params=_cparams(2),
        name="s_attn",
    )(page_table, q_ht, kn_sh, vn_sh, bias_rows, tri, *([cache_k_rows] * n_pg), *([cache_v_rows] * n_pg))


def _s_post_kernel(x_ref, sb_ref, u_ref, gv_ref, coef_ref, bt_ref, gt1_ref, sh2_ref, sc2_ref, gsb_ref, gml_ref,
                   wout_ref, g2_ref, wrt_ref, brt_ref, xm_ref, h2_ref, lg_ref, *, t_new):
    gv = gv_ref[...]
    mixed = bt_ref[...] + coef_ref[0] * gv
    for dd in range(1, t_new):
        mixed = mixed + coef_ref[dd] * pltpu.roll(gv, dd, 0)
    mlp = u_ref[...] * mixed
    _mix_tail(x_ref[...], sb_ref[...], mlp, gt1_ref[...], sh2_ref[...], sc2_ref[...], gsb_ref[...], gml_ref[...],
              wout_ref[...], g2_ref[...], wrt_ref[...], brt_ref[...], xm_ref, h2_ref, lg_ref)


def _s_post(x2, sb2, u2, gv2, coef, bias_t, gt1, sh2, sc2, gsb, gml, wout_bf, g2, wrt_bf, brt_b, t_new):
    n, d = x2.shape
    e = wrt_bf.shape[0]
    args = (x2, sb2, u2, gv2, coef, bias_t, gt1, sh2, sc2, gsb, gml, wout_bf, g2, wrt_bf, brt_b)
    full = lambda shape: pl.BlockSpec(shape, lambda i: (0,) * len(shape))
    return pl.pallas_call(
        functools.partial(_s_post_kernel, t_new=t_new),
        grid=(1,),
        in_specs=[full(a.shape) for a in args],
        out_specs=[full((n, d)), full((n * (d // LANES), LANES)), full((1, e, n))],
        out_shape=[jax.ShapeDtypeStruct((n, d), F32), jax.ShapeDtypeStruct((n * (d // LANES), LANES), F32),
                   jax.ShapeDtypeStruct((1, e, n), F32)],
        compiler_params=_cparams(1),
        name="s_post",
    )(*args)


def _route_kernel(lg_ref, tri_ref, e_ref, p_ref, r_ref, cnt_ref, base_ref):
    first = jnp.logical_and(pl.program_id(0) == 0, pl.program_id(1) == 0)

    @pl.when(first)
    def _():
        base_ref[...] = jnp.zeros_like(base_ref)

    lg = lg_ref[0]
    n_e, tr = lg.shape
    eidx = lax.broadcasted_iota(I32, (n_e, tr), 0)
    work = lg
    cnt = jnp.zeros((n_e, tr), F32)
    sels, vals = [], []
    for kk in range(TOP_K):
        m = jnp.max(work, axis=0, keepdims=True)
        ek = jnp.min(jnp.where(work == m, eidx, n_e), axis=0, keepdims=True)
        sel = eidx == ek
        work = jnp.where(sel, -jnp.inf, work)
        cnt = cnt + sel.astype(F32)
        e_ref[0, kk:kk + 1, :] = ek
        sels.append(sel)
        vals.append(m)
    ex = [jnp.exp(vv - vals[0]) for vv in vals]
    den = ex[0] + ex[1] + ex[2] + ex[3]
    for kk in range(TOP_K):
        p_ref[0, kk:kk + 1, :] = ex[kk] / den
    before = jnp.dot(cnt.astype(BF16), tri_ref[...], preferred_element_type=F32)
    base = base_ref[...]
    pos = before + jnp.concatenate([base] * (tr // LANES), axis=1)
    for kk in range(TOP_K):
        r_ref[0, kk:kk + 1, :] = jnp.sum(jnp.where(sels[kk], pos, 0.0), axis=0, keepdims=True).astype(I32)
    base = base + jnp.sum(cnt, axis=1, keepdims=True)
    base_ref[...] = base
    cnt_ref[...] = base.astype(I32)


def _route(lgt):
    b, n_e, s = lgt.shape
    tr = min(TR_ROUTE, s)
    t1 = lax.broadcasted_iota(I32, (tr, tr), 0)
    t2 = lax.broadcasted_iota(I32, (tr, tr), 1)
    tri = (t1 < t2).astype(BF16)
    tok = pl.BlockSpec((1, TOP_K, tr), lambda bb, ii: (bb, 0, ii))
    return pl.pallas_call(
        _route_kernel,
        grid=(b, s // tr),
        in_specs=[pl.BlockSpec((1, n_e, tr), lambda bb, ii: (bb, 0, ii)),
                  pl.BlockSpec((tr, tr), lambda bb, ii: (0, 0))],
        out_specs=[tok, tok, tok, pl.BlockSpec((n_e, LANES), lambda bb, ii: (0, 0))],
        out_shape=[jax.ShapeDtypeStruct((b, TOP_K, s), I32), jax.ShapeDtypeStruct((b, TOP_K, s), F32),
                   jax.ShapeDtypeStruct((b, TOP_K, s), I32), jax.ShapeDtypeStruct((n_e, LANES), I32)],
        scratch_shapes=[pltpu.VMEM((n_e, LANES), F32)],
        compiler_params=_cparams(2),
        name="route",
    )(lgt, tri)


def _slot_kernel(ps_ref, e_ref, r_ref, s_ref, *, n_e):
    e = e_ref[...]
    slot = r_ref[...]
    for ee in range(n_e):
        slot = slot + jnp.where(e == ee, ps_ref[ee], 0)
    s_ref[...] = slot


def _slots(pad_start, e_t, r_t, n_e):
    b, kk, s = e_t.shape
    blk = pl.BlockSpec((1, kk, s), lambda bb, ps: (bb, 0, 0))
    return pl.pallas_call(
        functools.partial(_slot_kernel, n_e=n_e),
        grid_spec=pltpu.PrefetchScalarGridSpec(num_scalar_prefetch=1, grid=(b,), in_specs=[blk, blk],
                                               out_specs=blk),
        out_shape=jax.ShapeDtypeStruct((b, kk, s), I32),
        compiler_params=_cparams(1),
        name="slots",
    )(pad_start, e_t, r_t)


def _row_copy(src_ref, src_row, dst_ref, dst_row, sem):
    return pltpu.make_async_copy(src_ref.at[pl.ds(pl.multiple_of(src_row * ROW_TILE, ROW_TILE), ROW_TILE)],
                                 dst_ref.at[pl.ds(pl.multiple_of(dst_row * ROW_TILE, ROW_TILE), ROW_TILE)], sem)


def _dispatch_kernel(ps_ref, sz_ref, pd_ref, nu_ref, slot_ref, h_ref, xs_ref, zero_ref, sem, zsem, *, td, n_e, bm,
                     n_blocks):
    first = jnp.logical_and(pl.program_id(0) == 0, pl.program_id(1) == 0)

    @pl.when(first)
    def _():
        zero_ref[...] = jnp.zeros_like(zero_ref)

        def tail_block(blk, carry):
            cp = pltpu.make_async_copy(
                zero_ref, xs_ref.at[pl.ds(pl.multiple_of(blk * (bm * ROW_TILE), ROW_TILE), bm * ROW_TILE)], zsem)
            cp.start()
            cp.wait()
            return carry

        lax.fori_loop(nu_ref[0], n_blocks, tail_block, 0)

        def per_expert(ee, carry):
            def start(r, c):
                _row_copy(zero_ref, 0, xs_ref, ps_ref[ee] + r, zsem).start()
                return c

            def wait(r, c):
                _row_copy(zero_ref, 0, xs_ref, ps_ref[ee] + r, zsem).wait()
                return c

            lax.fori_loop(sz_ref[ee], pd_ref[ee], start, 0)
            lax.fori_loop(sz_ref[ee], pd_ref[ee], wait, 0)
            return carry

        lax.fori_loop(0, n_e, per_expert, 0)

    def start(t, c):
        for kk in range(TOP_K):
            _row_copy(h_ref, t, xs_ref, slot_ref[0, kk, t], sem).start(priority=kk % 2)
        return c

    lax.fori_loop(0, td, start, 0)
    for kk in range(TOP_K):
        pltpu.make_async_copy(h_ref, xs_ref.at[pl.ds(0, td * ROW_TILE)], sem).wait()


def _dispatch(pad_start, sizes, padded, n_used, slot_t, h2_rows, n_blocks, bm):
    b, kk, s = slot_t.shape
    td = min(TD_DISPATCH, s)
    n_i = s // td
    n_e = sizes.shape[0]
    n_slots = n_blocks * bm
    grid_spec = pltpu.PrefetchScalarGridSpec(
        num_scalar_prefetch=4,
        grid=(b, n_i),
        in_specs=[pl.BlockSpec((1, kk, td), lambda bb, ii, *_: (bb, 0, ii), memory_space=pltpu.SMEM),
                  pl.BlockSpec((td * ROW_TILE, LANES), lambda bb, ii, *_: (bb * n_i + ii, 0))],
        out_specs=pl.BlockSpec(memory_space=pl.ANY),
        scratch_shapes=[pltpu.VMEM((bm * ROW_TILE, LANES), F32), pltpu.SemaphoreType.DMA(()),
                        pltpu.SemaphoreType.DMA(())])
    return pl.pallas_call(
        functools.partial(_dispatch_kernel, td=td, n_e=n_e, bm=bm, n_blocks=n_blocks),
        grid_spec=grid_spec,
        out_shape=jax.ShapeDtypeStruct((n_slots * ROW_TILE, LANES), F32),
        compiler_params=_cparams(2),
        name="dispatch",
    )(pad_start, sizes, padded, n_used, slot_t, h2_rows)


def _expert_kernel(be_ref, nu_ref, xs_ref, wgu_ref, bgu_ref, wd_ref, bd_ref, yb_ref, *, bm, d, d_ff):
    blk = pl.program_id(0)
    n_c = d // LANES

    @pl.when(blk < nu_ref[0])
    def _():
        x = jnp.concatenate([xs_ref[pl.ds(c, bm, stride=n_c), :] for c in range(n_c)], axis=1)
        gu = jnp.dot(x.astype(BF16), wgu_ref[0], preferred_element_type=F32) + bgu_ref[0]
        gate = jnp.minimum(gu[:, :d_ff], SWIGLU_LIMIT)
        up = jnp.clip(gu[:, d_ff:], -SWIGLU_LIMIT, SWIGLU_LIMIT)
        act = gate * (1.0 / (1.0 + jnp.exp(-SWIGLU_ALPHA * gate))) * (up + 1.0)
        y = jnp.dot(act.astype(BF16), wd_ref[0], preferred_element_type=F32) + bd_ref[0]
        for c in range(n_c):
            yb_ref[pl.ds(c, bm, stride=n_c), :] = y[:, c * LANES:(c + 1) * LANES]

    @pl.when(blk >= nu_ref[0])
    def _():
        yb_ref[...] = jnp.zeros_like(yb_ref)


def _experts(block_e, n_used, xs_rows, wgu_bf, bgu, wd_bf, bd, n_blocks, bm):
    n_e, d, d_ff2 = wgu_bf.shape
    d_ff = d_ff2 // 2
    grid_spec = pltpu.PrefetchScalarGridSpec(
        num_scalar_prefetch=2,
        grid=(n_blocks,),
        in_specs=[pl.BlockSpec((bm * ROW_TILE, LANES), lambda i, be, nu: (i, 0)),
                  pl.BlockSpec((1, d, d_ff2), lambda i, be, nu: (be[i], 0, 0)),
                  pl.BlockSpec((1, 1, d_ff2), lambda i, be, nu: (be[i], 0, 0)),
                  pl.BlockSpec((1, d_ff, d), lambda i, be, nu: (be[i], 0, 0)),
                  pl.BlockSpec((1, 1, d), lambda i, be, nu: (be[i], 0, 0))],
        out_specs=pl.BlockSpec((bm * ROW_TILE, LANES), lambda i, be, nu: (i, 0)))
    return pl.pallas_call(
        functools.partial(_expert_kernel, bm=bm, d=d, d_ff=d_ff),
        grid_spec=grid_spec,
        out_shape=jax.ShapeDtypeStruct((n_blocks * bm * ROW_TILE, LANES), F32),
        compiler_params=_cparams(1),
        name="experts",
    )(block_e, n_used, xs_rows, wgu_bf, bgu.reshape(n_e, 1, d_ff2), wd_bf, bd.reshape(n_e, 1, d))


def _combine_kernel(slot_ref, p_ref, xm_ref, gt_ref, gf_ref, yb_ref, y_ref, gbuf, sem, *, tc, d):
    n_c = d // LANES

    def start(t, c):
        for kk in range(TOP_K):
            _row_copy(yb_ref, slot_ref[0, kk, t], gbuf, kk * tc + t, sem).start(priority=kk % 2)
        return c

    lax.fori_loop(0, tc, start, 0)
    pltpu.make_async_copy(yb_ref.at[pl.ds(0, TOP_K * tc * ROW_TILE)], gbuf, sem).wait()
    p = p_ref[0]
    xm = xm_ref[0]
    gt2 = gt_ref[0]
    cols = []
    for c in range(n_c):
        acc = p[:, 0:1] * gbuf[pl.ds(c, tc, stride=n_c), :]
        for kk in range(1, TOP_K):
            acc = acc + p[:, kk:kk + 1] * gbuf[pl.ds(kk * tc * n_c + c, tc, stride=n_c), :]
        cols.append(acc)
    y = jnp.concatenate(cols, axis=1)
    xo = xm + gt2 * y
    y_ref[0] = _rms(xo) * gf_ref[...]


def _combine(slot_t, p_tok, xm, gt2, gfin, yb_rows):
    b, s, d = xm.shape
    tc = min(TC_COMBINE, s)
    n_i = s // tc
    kk = slot_t.shape[1]
    if gt2.shape[1] == 1:
        gt_spec = pl.BlockSpec((1, 1, d), lambda bb, ii: (bb, 0, 0))
    else:
        gt_spec = pl.BlockSpec((1, tc, d), lambda bb, ii: (bb, ii, 0))
    return pl.pallas_call(
        functools.partial(_combine_kernel, tc=tc, d=d),
        grid=(b, n_i),
        in_specs=[pl.BlockSpec((1, kk, tc), lambda bb, ii: (bb, 0, ii), memory_space=pltpu.SMEM),
                  pl.BlockSpec((1, tc, kk), lambda bb, ii: (bb, ii, 0)),
                  pl.BlockSpec((1, tc, d), lambda bb, ii: (bb, ii, 0)),
                  gt_spec,
                  pl.BlockSpec((1, d), lambda bb, ii: (0, 0)),
                  pl.BlockSpec(memory_space=pl.ANY)],
        out_specs=pl.BlockSpec((1, tc, d), lambda bb, ii: (bb, ii, 0)),
        out_shape=jax.ShapeDtypeStruct((b, s, d), F32),
        scratch_shapes=[pltpu.VMEM((TOP_K * tc * ROW_TILE, LANES), F32), pltpu.SemaphoreType.DMA(())],
        compiler_params=_cparams(2),
        name="combine",
    )(slot_t, p_tok, xm, gt2, gfin, yb_rows)


def _moe(lgt, h2_rows, xm, gt2, gfin, wgu_bf, bgu, wd_bf, bd):
    b, n_e, s = lgt.shape
    n_assign = b * s * TOP_K
    bm = BM_EXPERT if n_assign >= n_e * BM_EXPERT else BM_EXPERT_SMALL
    n_blocks = -(-n_assign // bm) + n_e
    n_slots = n_blocks * bm
    e_t, p_t, r_t, cnt = _route(lgt)
    sizes = cnt[:, 0]
    padded = ((sizes + bm - 1) // bm) * bm
    pad_end = jnp.cumsum(padded)
    pad_start = pad_end - padded
    block_start = jnp.arange(n_blocks, dtype=I32) * bm
    block_e = jnp.minimum(jnp.sum((pad_end[None, :] <= block_start[:, None]).astype(I32), axis=1), n_e - 1)
    n_used = (pad_end[-1:] // bm).astype(I32)
    slot_t = _slots(pad_start, e_t, r_t, n_e)
    xs_rows = _dispatch(pad_start, sizes, padded, n_used, slot_t, h2_rows, n_blocks, bm)
    yb_rows = _experts(block_e, n_used, xs_rows, wgu_bf, bgu, wd_bf, bd, n_blocks, bm)
    return _combine(slot_t, jnp.transpose(p_t, (0, 2, 1)), xm, gt2, gfin, yb_rows)


def kernel(x_prompt, x_sample, cache_k, cache_v, page_table, c_prompt, c_sample, w_ada, b_ada, g_norm_mix,
           g_norm_ffn, w_in, b_sb, w_s, b_s, g_sb_out, g_mlp_out, w_out, w_router, b_router, w_gate_up,
           b_gate_up, w_down, b_down, g_final):
    b, s, d = x_prompt.shape
    bd, t_new, _ = x_sample.shape
    depth, n_pool, page, h_sb, d_head = cache_k.shape
    assert depth == 1 and d_head == D_HEAD
    d_sb = h_sb * d_head
    d_mlp = g_mlp_out.shape[1]
    g_mlp = d_mlp // d_head
    n_e = w_router.shape[2]
    chunk = w_s.shape[2]
    n_tok_s = bd * t_new

    win_bf = w_in[0].astype(BF16)
    wout_bf = w_out[0].astype(BF16)
    wada_bf = w_ada[0].astype(BF16)
    wrt_bf = jnp.transpose(w_router[0]).astype(BF16)
    wgu_bf = w_gate_up[0].astype(BF16)
    wd_bf = w_down[0].astype(BF16)
    brt_b = jnp.broadcast_to(b_router[0][:, None], (n_e, LANES))
    g1 = g_norm_mix[0].reshape(1, d)
    g2 = g_norm_ffn[0].reshape(1, d)
    gsb = g_sb_out[0].reshape(1, d_sb)
    gml = g_mlp_out[0].reshape(1, d_mlp)
    gfin = g_final.reshape(1, d)
    bs_full = jnp.repeat(jnp.transpose(b_s[0]), d_head, axis=1)
    bias_rows = jnp.broadcast_to(jnp.repeat(b_sb[0], t_new)[:, None], (h_sb * t_new, LANES))

    ada = _ada(jnp.concatenate([c_prompt, c_sample], axis=0), wada_bf, b_ada[0])
    ada_p = ada[:b].reshape(b, N_ADA, d)
    ada_s = jnp.repeat(ada[b:].reshape(bd, N_ADA, d), t_new, axis=0)

    k_p, v_p, xm_p, h2_p, lg_p = _mix_prompt(x_prompt, ada_p, b_sb[0], g1, win_bf, w_s[0], bs_full, gsb, gml,
                                             wout_bf, g2, wrt_bf, brt_b)
    y_prompt = _moe(lg_p, h2_p, xm_p, ada_p[:, 5:6, :], gfin, wgu_bf, b_gate_up[0], wd_bf, b_down[0])

    x_s2 = x_sample.reshape(n_tok_s, d)
    q_s, k_s, v_s, u_s, gv_s = _s_pre(x_s2, ada_s[:, 0], ada_s[:, 1], g1, win_bf, d_sb, d_mlp)
    q_ht = jnp.transpose(q_s.reshape(bd, t_new, h_sb, d_head), (0, 2, 1, 3)).reshape(bd, h_sb * t_new, d_head)
    sb_ht = _s_attn(q_ht, k_s.reshape(bd, t_new * h_sb, d_head), v_s.reshape(bd, t_new * h_sb, d_head), bias_rows,
                    cache_k.reshape(n_pool, page * h_sb, d_head), cache_v.reshape(n_pool, page * h_sb, d_head),
                    page_table, t_new)
    sb_s = jnp.transpose(sb_ht.reshape(bd, h_sb, t_new, d_head), (0, 2, 1, 3))
    w_tril = jnp.tril(w_s[0][:, :t_new, :t_new])
    tt = jnp.arange(t_new)
    coef = jnp.stack([jnp.where((tt >= dd)[None, :], w_tril[:, tt, jnp.maximum(tt - dd, 0)], 0.0)
                      for dd in range(t_new)])
    coef = jnp.tile(jnp.repeat(jnp.transpose(coef, (0, 2, 1)), d_head, axis=2), (1, bd, 1))
    bias_t = jnp.tile(jnp.repeat(jnp.transpose(b_s[0][:, :t_new]), d_head, axis=1), (bd, 1))
    xm_s, h2_s, lg_s = _s_post(x_s2, sb_s.reshape(n_tok_s, d_sb), u_s, gv_s, coef, bias_t, ada_s[:, 2],
                               ada_s[:, 3], ada_s[:, 4], gsb, gml, wout_bf, g2, wrt_bf, brt_b, t_new)
    y_sample = _moe(lg_s, h2_s, xm_s.reshape(1, n_tok_s, d), ada_s[:, 5].reshape(1, n_tok_s, d), gfin,
                    wgu_bf, b_gate_up[0], wd_bf, b_down[0])

    return (y_prompt, y_sample.reshape(bd, t_new, d),
            k_p.reshape(1, b, s // page, page, h_sb, d_head), v_p.reshape(1, b, s // page, page, h_sb, d_head),
            k_s.reshape(1, bd, t_new, h_sb, d_head), v_s.reshape(1, bd, t_new, h_sb, d_head),
            gv_s.reshape(1, bd, t_new, d_mlp))
```

```python
import functools
import math

import jax
import jax.numpy as jnp
from jax import lax
from jax.experimental import pallas as pl
from jax.experimental.pallas import tpu as pltpu

F32 = jnp.float32
BF16 = jnp.bfloat16
I32 = jnp.int32

D_HEAD = 64
TOP_K = 4
N_ADA = 6
EPS = 1e-6
SWIGLU_LIMIT = 7.0
SWIGLU_ALPHA = 1.702
LANES = 128
SUBLANES = 8
ROW_TILE = SUBLANES
VMEM_LIMIT = 56 * 1024 * 1024

TS_MIX = 256
TR_ROUTE = 512
TD_DISPATCH = 512
BM_EXPERT = 512
BM_EXPERT_SMALL = 128
TC_COMBINE = 256
PAGES_PER_STEP = 16


def _cparams(n_axes):
    return pltpu.CompilerParams(dimension_semantics=("arbitrary",) * n_axes, vmem_limit_bytes=VMEM_LIMIT)


def _rms(x):
    return x * lax.rsqrt(jnp.mean(x * x, axis=-1, keepdims=True) + EPS)


def _gelu(x):
    c = math.sqrt(2.0 / math.pi)
    return x * (0.5 * (1.0 + jnp.tanh(c * (x + 0.044715 * (x * x * x)))))


def _log_sigmoid_pair(z):
    t = jnp.log(1.0 + jnp.exp(-jnp.abs(z)))
    lsig = jnp.minimum(z, 0.0) - t
    return lsig, lsig - z


def _sb_weights(z, crun, triu, mask):
    lsig, ls = _log_sigmoid_pair(z)
    if mask is not None:
        ls = jnp.where(mask, ls, 0.0)
    after = jnp.dot(ls.astype(BF16), triu, preferred_element_type=F32)
    cr = jnp.concatenate([crun] * (z.shape[1] // LANES), axis=1) if z.shape[1] > LANES else crun
    a = jnp.exp(lsig + after + cr)
    if mask is not None:
        a = jnp.where(mask, a, 0.0)
    return a, crun + jnp.sum(ls, axis=1, keepdims=True)


def _ada_kernel(c_ref, w_ref, b_ref, o_ref):
    c = c_ref[...]
    s = c * (1.0 / (1.0 + jnp.exp(-c)))
    o_ref[...] = jnp.dot(s.astype(BF16), w_ref[...], preferred_element_type=F32) + b_ref[...]


def _ada(c_all, w_ada_bf, b_ada):
    r, d = c_all.shape
    n = w_ada_bf.shape[1]
    return pl.pallas_call(
        _ada_kernel,
        grid=(n // d,),
        in_specs=[pl.BlockSpec((r, d), lambda j: (0, 0)),
                  pl.BlockSpec((d, d), lambda j: (0, j)),
                  pl.BlockSpec((1, d), lambda j: (0, j))],
        out_specs=pl.BlockSpec((r, d), lambda j: (0, j)),
        out_shape=jax.ShapeDtypeStruct((r, n), F32),
        compiler_params=_cparams(1),
        name="ada",
    )(c_all, w_ada_bf, b_ada.reshape(1, n))


def _mix_tail(x, sb, mlp, gt1, sh2, sc2, gsb, gml, wout, g2, wrt, brt, xm_ref, h2_ref, lg_ref):
    rows, d = x.shape
    o = jnp.concatenate([_rms(sb) * gsb, _rms(mlp) * gml], axis=1).astype(BF16)
    xm = x + gt1 * jnp.dot(o, wout, preferred_element_type=F32)
    xm_ref[...] = xm.reshape(xm_ref.shape)
    h2 = _rms(xm) * g2 * (1.0 + sc2) + sh2
    for c in range(d // LANES):
        h2_ref[pl.ds(c, rows, stride=d // LANES), :] = h2[:, c * LANES:(c + 1) * LANES]
    lg = lax.dot_general(wrt, h2.astype(BF16), (((1,), (1,)), ((), ())), preferred_element_type=F32)
    lg = lg + jnp.concatenate([brt] * (rows // LANES), axis=1)
    lg_ref[...] = lg.reshape(lg_ref.shape)


def _mix_prompt_kernel(bsb_ref, x_ref, ada_ref, g1_ref, win_ref, ws_ref, bsf_ref, triu_ref, gsb_ref, gml_ref,
                       wout_ref, g2_ref, wrt_ref, brt_ref,
                       k_ref, v_ref, xm_ref, h2_ref, lg_ref,
                       kt_buf, v_buf, o_acc, c_acc, *, ts, n_pairs, d_sb, d_mlp):
    i = pl.program_id(1)
    x = x_ref[0]
    ada = ada_ref[0]
    sh1, sc1, gt1, sh2, sc2 = ada[0:1], ada[1:2], ada[2:3], ada[3:4], ada[4:5]
    h = _rms(x) * g1_ref[...] * (1.0 + sc1) + sh1
    p = jnp.dot(h.astype(BF16), win_ref[...], preferred_element_type=F32)
    q = p[:, :d_sb] * (D_HEAD ** -0.5)
    k = p[:, d_sb:2 * d_sb]
    v = p[:, 2 * d_sb:3 * d_sb]
    u = _gelu(p[:, 3 * d_sb:3 * d_sb + d_mlp])
    gv = _gelu(p[:, 3 * d_sb + d_mlp:])
    kt = jnp.transpose(k)
    vt = jnp.transpose(v)
    page = k_ref.shape[3]
    for r in range(ts // page):
        k_ref[0, r] = kt[:, r * page:(r + 1) * page]
        v_ref[0, r] = vt[:, r * page:(r + 1) * page]
    kt_buf[i] = kt.astype(BF16)
    v_buf[i] = v.astype(BF16)

    lane = lax.broadcasted_iota(I32, (ts, LANES), 1)
    lo = lane < D_HEAD
    row = lax.broadcasted_iota(I32, (2 * ts, ts), 0)
    col = lax.broadcasted_iota(I32, (2 * ts, ts), 1)
    diag_mask = col < jnp.where(row >= ts, row - ts, row)
    triu = triu_ref[...]
    qs = []
    for j in range(n_pairs):
        q2 = q[:, j * LANES:(j + 1) * LANES]
        qs.append(jnp.concatenate([jnp.where(lo, q2, 0.0), jnp.where(lo, 0.0, q2)], axis=0).astype(BF16))

    def pair_unit(j, c, crun, mask):
        kt = kt_buf[c, j * LANES:(j + 1) * LANES, :]
        vv = v_buf[c, :, j * LANES:(j + 1) * LANES]
        z = jnp.dot(qs[j], kt, preferred_element_type=F32)
        z = jnp.concatenate([z[:ts] + bsb_ref[2 * j], z[ts:] + bsb_ref[2 * j + 1]], axis=0)
        a, crun = _sb_weights(z, crun, triu, mask)
        return jnp.dot(a.astype(BF16), vv, preferred_element_type=F32), crun

    for j in range(n_pairs):
        o, crun = pair_unit(j, i, jnp.zeros((2 * ts, LANES), F32), diag_mask)
        o_acc[j] = o
        c_acc[j] = crun

    def chunk_body(it, carry):
        c = i - 1 - it
        for j in range(n_pairs):
            o, crun = pair_unit(j, c, c_acc[j], None)
            o_acc[j] = o_acc[j] + o
            c_acc[j] = crun
        return carry

    lax.fori_loop(0, i, chunk_body, 0)
    sb = jnp.concatenate([jnp.where(lo, o_acc[j, :ts], o_acc[j, ts:]) for j in range(n_pairs)], axis=1)

    chunk = ws_ref.shape[1]
    tr = lax.broadcasted_iota(I32, (chunk, chunk), 0)
    tc = lax.broadcasted_iota(I32, (chunk, chunk), 1)
    tril = tc <= tr
    lo_c = lax.broadcasted_iota(I32, (chunk, LANES), 1) < D_HEAD
    ws = [jnp.where(tril, ws_ref[g], 0.0).astype(BF16) for g in range(2 * n_pairs)]
    mlp_rows = []
    for r in range(ts // chunk):
        cols = []
        for j in range(n_pairs):
            gv2 = gv[r * chunk:(r + 1) * chunk, j * LANES:(j + 1) * LANES].astype(BF16)
            ma = jnp.dot(ws[2 * j], gv2, preferred_element_type=F32)
            mb = jnp.dot(ws[2 * j + 1], gv2, preferred_element_type=F32)
            mixed = jnp.where(lo_c, ma, mb) + bsf_ref[:, j * LANES:(j + 1) * LANES]
            cols.append(u[r * chunk:(r + 1) * chunk, j * LANES:(j + 1) * LANES] * mixed)
        mlp_rows.append(jnp.concatenate(cols, axis=1))
    mlp = jnp.concatenate(mlp_rows, axis=0)

    _mix_tail(x, sb, mlp, gt1, sh2, sc2, gsb_ref[...], gml_ref[...], wout_ref[...], g2_ref[...],
              wrt_ref[...], brt_ref[...], xm_ref, h2_ref, lg_ref)


def _mix_prompt(x, ada_p, b_sb, g1, win_bf, w_s, bs_full, gsb, gml, wout_bf, g2, wrt_bf, brt_b, page):
    b, s, d = x.shape
    ts = min(TS_MIX, s)
    n_i = s // ts
    d_sb = gsb.shape[1]
    d_mlp = gml.shape[1]
    n_pairs = d_sb // LANES
    e = wrt_bf.shape[0]
    jj = lax.broadcasted_iota(I32, (ts, ts), 0)
    ss = lax.broadcasted_iota(I32, (ts, ts), 1)
    triu = (jj > ss).astype(BF16)
    const = lambda shape: pl.BlockSpec(shape, lambda bb, ii, *_: (0,) * len(shape))
    grid_spec = pltpu.PrefetchScalarGridSpec(
        num_scalar_prefetch=1,
        grid=(b, n_i),
        in_specs=[pl.BlockSpec((1, ts, d), lambda bb, ii, *_: (bb, ii, 0)),
                  pl.BlockSpec((1, N_ADA, d), lambda bb, ii, *_: (bb, 0, 0)),
                  const((1, d)), const(win_bf.shape), const(w_s.shape), const(bs_full.shape), const((ts, ts)),
                  const((1, d_sb)), const((1, d_mlp)), const(wout_bf.shape), const((1, d)),
                  const(wrt_bf.shape), const(brt_b.shape)],
        out_specs=[pl.BlockSpec((1, ts // page, d_sb, page), lambda bb, ii, *_: (bb, ii, 0, 0)),
                   pl.BlockSpec((1, ts // page, d_sb, page), lambda bb, ii, *_: (bb, ii, 0, 0)),
                   pl.BlockSpec((1, ts, d), lambda bb, ii, *_: (bb, ii, 0)),
                   pl.BlockSpec((ts * (d // LANES), LANES), lambda bb, ii, *_: (bb * n_i + ii, 0)),
                   pl.BlockSpec((1, e, ts), lambda bb, ii, *_: (bb, 0, ii))],
        scratch_shapes=[pltpu.VMEM((n_i, d_sb, ts), BF16), pltpu.VMEM((n_i, ts, d_sb), BF16),
                        pltpu.VMEM((n_pairs, 2 * ts, LANES), F32), pltpu.VMEM((n_pairs, 2 * ts, LANES), F32)])
    return pl.pallas_call(
        functools.partial(_mix_prompt_kernel, ts=ts, n_pairs=n_pairs, d_sb=d_sb, d_mlp=d_mlp),
        grid_spec=grid_spec,
        out_shape=[jax.ShapeDtypeStruct((b, s // page, d_sb, page), F32),
                   jax.ShapeDtypeStruct((b, s // page, d_sb, page), F32),
                   jax.ShapeDtypeStruct((b, s, d), F32),
                   jax.ShapeDtypeStruct((b * s * (d // LANES), LANES), F32),
                   jax.ShapeDtypeStruct((b, e, s), F32)],
        compiler_params=_cparams(2),
        name="mix_prompt",
    )(b_sb, x, ada_p, g1, win_bf, w_s, bs_full, triu, gsb, gml, wout_bf, g2, wrt_bf, brt_b)


def _s_pre_kernel(x_ref, sh_ref, sc_ref, g1_ref, win_ref, q_ref, k_ref, v_ref, u_ref, gv_ref, *, d_sb, d_mlp):
    h = _rms(x_ref[...]) * g1_ref[...] * (1.0 + sc_ref[...]) + sh_ref[...]
    p = jnp.dot(h.astype(BF16), win_ref[...], preferred_element_type=F32)
    q_ref[...] = p[:, :d_sb] * (D_HEAD ** -0.5)
    k_ref[...] = p[:, d_sb:2 * d_sb]
    v_ref[...] = p[:, 2 * d_sb:3 * d_sb]
    u_ref[...] = _gelu(p[:, 3 * d_sb:3 * d_sb + d_mlp])
    gv_ref[...] = _gelu(p[:, 3 * d_sb + d_mlp:])


def _s_pre(x2, sh1, sc1, g1, win_bf, d_sb, d_mlp):
    n, d = x2.shape
    full = lambda shape: pl.BlockSpec(shape, lambda i: (0,) * len(shape))
    return pl.pallas_call(
        functools.partial(_s_pre_kernel, d_sb=d_sb, d_mlp=d_mlp),
        grid=(1,),
        in_specs=[full((n, d)), full((n, d)), full((n, d)), full((1, d)), full(win_bf.shape)],
        out_specs=[full((n, d_sb))] * 3 + [full((n, d_mlp))] * 2,
        out_shape=[jax.ShapeDtypeStruct((n, d_sb), F32)] * 3 + [jax.ShapeDtypeStruct((n, d_mlp), F32)] * 2,
        compiler_params=_cparams(1),
        name="s_pre",
    )(x2, sh1, sc1, g1, win_bf)


def _sb_columns(z, crun, tri, mask, out_fn):
    gw = tri.shape[0]
    n_g = z.shape[1] // gw
    rows = z.shape[0]
    lsig, ls = _log_sigmoid_pair(z)
    if mask is not None:
        ls = jnp.where(mask, ls, 0.0)
    parts = [ls[:, g * gw:(g + 1) * gw] for g in range(n_g)]
    aft = jnp.dot(jnp.concatenate(parts, axis=0).astype(BF16), tri, preferred_element_type=F32)
    cols = [None] * n_g
    for g in reversed(range(n_g)):
        cols[g] = aft[g * rows:(g + 1) * rows] + jnp.concatenate([crun] * (gw // LANES), axis=1)
        crun = crun + jnp.sum(parts[g], axis=1, keepdims=True)
    a = jnp.exp(lsig + jnp.concatenate(cols, axis=1))
    if mask is not None:
        a = jnp.where(mask, a, 0.0)
    return out_fn(a.astype(BF16)), crun


def _s_attn_kernel(pt_ref, q_ref, kn_ref, vn_ref, bias_ref, tri_ref, *refs, n_pg, t_new, n_heads):
    k_refs = refs[:n_pg]
    v_refs = refs[n_pg:2 * n_pg]
    o_ref = refs[2 * n_pg]
    o_acc, c_acc = refs[2 * n_pg + 1:]
    jj = pl.program_id(1)
    rows = n_heads * t_new
    d_sb = q_ref.shape[2]
    page = k_refs[0].shape[2]
    q_rep = jnp.concatenate([q_ref[0]] * n_heads, axis=0)
    rr = lax.broadcasted_iota(I32, (rows, d_sb), 0)
    ll = lax.broadcasted_iota(I32, (rows, d_sb), 1)
    head_mask = (ll // D_HEAD) == (rr // t_new)
    qbd = jnp.where(head_mask, q_rep, 0.0).astype(BF16)
    bias = bias_ref[...]
    tri = tri_ref[...]
    nt = (((1,), (1,)), ((), ()))

    @pl.when(jj == 0)
    def _():
        pad = jnp.zeros((page - t_new, d_sb), F32)
        kb = jnp.concatenate([kn_ref[0], pad], axis=0).astype(BF16)
        vb = jnp.concatenate([vn_ref[0], pad], axis=0).astype(BF16)
        r2 = lax.broadcasted_iota(I32, (rows, page), 0)
        c2 = lax.broadcasted_iota(I32, (rows, page), 1)
        z = lax.dot_general(qbd, kb, nt, preferred_element_type=F32) + bias
        o, crun = _sb_columns(z, jnp.zeros((rows, LANES), F32), tri[:page, :page], c2 < (r2 % t_new),
                              lambda a: jnp.dot(a, vb, preferred_element_type=F32))
        o_acc[...] = o
        c_acc[...] = crun

    order = list(reversed(range(n_pg)))
    z = jnp.concatenate([jnp.dot(qbd, k_refs[pg][0].astype(BF16), preferred_element_type=F32) + bias
                         for pg in order], axis=1)

    def values(a):
        o = None
        for ci, pg in enumerate(order):
            od = lax.dot_general(a[:, ci * page:(ci + 1) * page], v_refs[pg][0].astype(BF16), nt,
                                 preferred_element_type=F32)
            o = od if o is None else o + od
        return o

    od, crun = _sb_columns(z, c_acc[...], tri, None, values)
    o = o_acc[...] + od
    o_acc[...] = o
    c_acc[...] = crun

    @pl.when(jj == pl.num_programs(1) - 1)
    def _():
        om = jnp.where(head_mask, o, 0.0)
        acc = om[0:t_new]
        for hh in range(1, n_heads):
            acc = acc + om[hh * t_new:(hh + 1) * t_new]
        o_ref[0] = acc


def _s_attn(q3, kn3, vn3, bias_rows, cache_kt, cache_vt, page_table):
    bd, t_new, d_sb = q3.shape
    n_pages = page_table.shape[1]
    page = cache_kt.shape[2]
    n_heads = d_sb // D_HEAD
    n_pg = min(PAGES_PER_STEP, n_pages)
    n_j = n_pages // n_pg
    rows = n_heads * t_new
    gw = min(2 * LANES, n_pg * page)
    jj = lax.broadcasted_iota(I32, (gw, gw), 0)
    ss = lax.broadcasted_iota(I32, (gw, gw), 1)
    tri = (jj > ss).astype(BF16)

    def page_spec(pg):
        return pl.BlockSpec((1, d_sb, page), lambda b, j, pt: (pt[b, n_pages - 1 - (j * n_pg + pg)], 0, 0))

    tok = pl.BlockSpec((1, t_new, d_sb), lambda b, j, pt: (b, 0, 0))
    grid_spec = pltpu.PrefetchScalarGridSpec(
        num_scalar_prefetch=1,
        grid=(bd, n_j),
        in_specs=[tok, tok, tok,
                  pl.BlockSpec((rows, LANES), lambda b, j, pt: (0, 0)),
                  pl.BlockSpec((gw, gw), lambda b, j, pt: (0, 0))]
                 + [page_spec(pg) for pg in range(n_pg)] * 2,
        out_specs=tok,
        scratch_shapes=[pltpu.VMEM((rows, d_sb), F32), pltpu.VMEM((rows, LANES), F32)])
    return pl.pallas_call(
        functools.partial(_s_attn_kernel, n_pg=n_pg, t_new=t_new, n_heads=n_heads),
        grid_spec=grid_spec,
        out_shape=jax.ShapeDtypeStruct((bd, t_new, d_sb), F32),
        compiler_params=_cparams(2),
        name="s_attn",
    )(page_table, q3, kn3, vn3, bias_rows, tri, *([cache_kt] * n_pg), *([cache_vt] * n_pg))


def _s_post_kernel(x_ref, sb_ref, u_ref, gv_ref, coef_ref, bt_ref, gt1_ref, sh2_ref, sc2_ref, gsb_ref, gml_ref,
                   wout_ref, g2_ref, wrt_ref, brt_ref, xm_ref, h2_ref, lg_ref, *, t_new):
    gv = gv_ref[...]
    mixed = bt_ref[...] + coef_ref[0] * gv
    for dd in range(1, t_new):
        mixed = mixed + coef_ref[dd] * pltpu.roll(gv, dd, 0)
    mlp = u_ref[...] * mixed
    _mix_tail(x_ref[...], sb_ref[...], mlp, gt1_ref[...], sh2_ref[...], sc2_ref[...], gsb_ref[...], gml_ref[...],
              wout_ref[...], g2_ref[...], wrt_ref[...], brt_ref[...], xm_ref, h2_ref, lg_ref)


def _s_post(x2, sb2, u2, gv2, coef, bias_t, gt1, sh2, sc2, gsb, gml, wout_bf, g2, wrt_bf, brt_b, t_new):
    n, d = x2.shape
    e = wrt_bf.shape[0]
    args = (x2, sb2, u2, gv2, coef, bias_t, gt1, sh2, sc2, gsb, gml, wout_bf, g2, wrt_bf, brt_b)
    full = lambda shape: pl.BlockSpec(shape, lambda i: (0,) * len(shape))
    return pl.pallas_call(
        functools.partial(_s_post_kernel, t_new=t_new),
        grid=(1,),
        in_specs=[full(a.shape) for a in args],
        out_specs=[full((n, d)), full((n * (d // LANES), LANES)), full((1, e, n))],
        out_shape=[jax.ShapeDtypeStruct((n, d), F32), jax.ShapeDtypeStruct((n * (d // LANES), LANES), F32),
                   jax.ShapeDtypeStruct((1, e, n), F32)],
        compiler_params=_cparams(1),
        name="s_post",
    )(*args)


def _route_kernel(lg_ref, tri_ref, e_ref, p_ref, r_ref, cnt_ref, base_ref):
    first = jnp.logical_and(pl.program_id(0) == 0, pl.program_id(1) == 0)

    @pl.when(first)
    def _():
        base_ref[...] = jnp.zeros_like(base_ref)

    lg = lg_ref[0]
    n_e, tr = lg.shape
    eidx = lax.broadcasted_iota(I32, (n_e, tr), 0)
    work = lg
    cnt = jnp.zeros((n_e, tr), F32)
    sels, vals = [], []
    for kk in range(TOP_K):
        m = jnp.max(work, axis=0, keepdims=True)
        ek = jnp.min(jnp.where(work == m, eidx, n_e), axis=0, keepdims=True)
        sel = eidx == ek
        work = jnp.where(sel, -jnp.inf, work)
        cnt = cnt + sel.astype(F32)
        e_ref[0, kk:kk + 1, :] = ek
        sels.append(sel)
        vals.append(m)
    ex = [jnp.exp(vv - vals[0]) for vv in vals]
    den = ex[0] + ex[1] + ex[2] + ex[3]
    for kk in range(TOP_K):
        p_ref[0, kk:kk + 1, :] = ex[kk] / den
    before = jnp.dot(cnt.astype(BF16), tri_ref[...], preferred_element_type=F32)
    base = base_ref[...]
    pos = before + jnp.concatenate([base] * (tr // LANES), axis=1)
    for kk in range(TOP_K):
        r_ref[0, kk:kk + 1, :] = jnp.sum(jnp.where(sels[kk], pos, 0.0), axis=0, keepdims=True).astype(I32)
    base = base + jnp.sum(cnt, axis=1, keepdims=True)
    base_ref[...] = base
    cnt_ref[...] = base.astype(I32)


def _route(lgt):
    b, n_e, s = lgt.shape
    tr = min(TR_ROUTE, s)
    t1 = lax.broadcasted_iota(I32, (tr, tr), 0)
    t2 = lax.broadcasted_iota(I32, (tr, tr), 1)
    tri = (t1 < t2).astype(BF16)
    tok = pl.BlockSpec((1, TOP_K, tr), lambda bb, ii: (bb, 0, ii))
    return pl.pallas_call(
        _route_kernel,
        grid=(b, s // tr),
        in_specs=[pl.BlockSpec((1, n_e, tr), lambda bb, ii: (bb, 0, ii)),
                  pl.BlockSpec((tr, tr), lambda bb, ii: (0, 0))],
        out_specs=[tok, tok, tok, pl.BlockSpec((n_e, LANES), lambda bb, ii: (0, 0))],
        out_shape=[jax.ShapeDtypeStruct((b, TOP_K, s), I32), jax.ShapeDtypeStruct((b, TOP_K, s), F32),
                   jax.ShapeDtypeStruct((b, TOP_K, s), I32), jax.ShapeDtypeStruct((n_e, LANES), I32)],
        scratch_shapes=[pltpu.VMEM((n_e, LANES), F32)],
        compiler_params=_cparams(2),
        name="route",
    )(lgt, tri)


def _slot_kernel(ps_ref, e_ref, r_ref, s_ref, *, n_e):
    e = e_ref[...]
    slot = r_ref[...]
    for ee in range(n_e):
        slot = slot + jnp.where(e == ee, ps_ref[ee], 0)
    s_ref[...] = slot


def _slots(pad_start, e_t, r_t, n_e):
    b, kk, s = e_t.shape
    blk = pl.BlockSpec((1, kk, s), lambda bb, ps: (bb, 0, 0))
    return pl.pallas_call(
        functools.partial(_slot_kernel, n_e=n_e),
        grid_spec=pltpu.PrefetchScalarGridSpec(num_scalar_prefetch=1, grid=(b,), in_specs=[blk, blk],
                                               out_specs=blk),
        out_shape=jax.ShapeDtypeStruct((b, kk, s), I32),
        compiler_params=_cparams(1),
        name="slots",
    )(pad_start, e_t, r_t)


def _row_copy(src_ref, src_row, dst_ref, dst_row, sem):
    return pltpu.make_async_copy(src_ref.at[pl.ds(pl.multiple_of(src_row * ROW_TILE, ROW_TILE), ROW_TILE)],
                                 dst_ref.at[pl.ds(pl.multiple_of(dst_row * ROW_TILE, ROW_TILE), ROW_TILE)], sem)


def _dispatch_kernel(ps_ref, sz_ref, pd_ref, nu_ref, slot_ref, h_ref, xs_ref, zero_ref, sem, zsem, *, td, n_e, bm,
                     n_blocks):
    first = jnp.logical_and(pl.program_id(0) == 0, pl.program_id(1) == 0)

    @pl.when(first)
    def _():
        zero_ref[...] = jnp.zeros_like(zero_ref)

        def tail_block(blk, carry):
            cp = pltpu.make_async_copy(
                zero_ref, xs_ref.at[pl.ds(pl.multiple_of(blk * (bm * ROW_TILE), ROW_TILE), bm * ROW_TILE)], zsem)
            cp.start()
            cp.wait()
            return carry

        lax.fori_loop(nu_ref[0], n_blocks, tail_block, 0)

        def per_expert(ee, carry):
            def start(r, c):
                _row_copy(zero_ref, 0, xs_ref, ps_ref[ee] + r, zsem).start()
                return c

            def wait(r, c):
                _row_copy(zero_ref, 0, xs_ref, ps_ref[ee] + r, zsem).wait()
                return c

            lax.fori_loop(sz_ref[ee], pd_ref[ee], start, 0)
            lax.fori_loop(sz_ref[ee], pd_ref[ee], wait, 0)
            return carry

        lax.fori_loop(0, n_e, per_expert, 0)

    def start(t, c):
        for kk in range(TOP_K):
            _row_copy(h_ref, t, xs_ref, slot_ref[0, kk, t], sem).start(priority=kk % 2)
        return c

    lax.fori_loop(0, td, start, 0)
    for kk in range(TOP_K):
        pltpu.make_async_copy(h_ref, xs_ref.at[pl.ds(0, td * ROW_TILE)], sem).wait()


def _dispatch(pad_start, sizes, padded, n_used, slot_t, h2_rows, n_blocks, bm):
    b, kk, s = slot_t.shape
    td = min(TD_DISPATCH, s)
    n_i = s // td
    n_e = sizes.shape[0]
    n_slots = n_blocks * bm
    grid_spec = pltpu.PrefetchScalarGridSpec(
        num_scalar_prefetch=4,
        grid=(b, n_i),
        in_specs=[pl.BlockSpec((1, kk, td), lambda bb, ii, *_: (bb, 0, ii), memory_space=pltpu.SMEM),
                  pl.BlockSpec((td * ROW_TILE, LANES), lambda bb, ii, *_: (bb * n_i + ii, 0))],
        out_specs=pl.BlockSpec(memory_space=pl.ANY),
        scratch_shapes=[pltpu.VMEM((bm * ROW_TILE, LANES), F32), pltpu.SemaphoreType.DMA(()),
                        pltpu.SemaphoreType.DMA(())])
    return pl.pallas_call(
        functools.partial(_dispatch_kernel, td=td, n_e=n_e, bm=bm, n_blocks=n_blocks),
        grid_spec=grid_spec,
        out_shape=jax.ShapeDtypeStruct((n_slots * ROW_TILE, LANES), F32),
        compiler_params=_cparams(2),
        name="dispatch",
    )(pad_start, sizes, padded, n_used, slot_t, h2_rows)


def _expert_kernel(be_ref, nu_ref, xs_ref, wgu_ref, bgu_ref, wd_ref, bd_ref, yb_ref, *, bm, d, d_ff):
    blk = pl.program_id(0)
    n_c = d // LANES

    @pl.when(blk < nu_ref[0])
    def _():
        x = jnp.concatenate([xs_ref[pl.ds(c, bm, stride=n_c), :] for c in range(n_c)], axis=1)
        gu = jnp.dot(x.astype(BF16), wgu_ref[0], preferred_element_type=F32) + bgu_ref[0]
        gate = jnp.minimum(gu[:, :d_ff], SWIGLU_LIMIT)
        up = jnp.clip(gu[:, d_ff:], -SWIGLU_LIMIT, SWIGLU_LIMIT)
        act = gate * (1.0 / (1.0 + jnp.exp(-SWIGLU_ALPHA * gate))) * (up + 1.0)
        y = jnp.dot(act.astype(BF16), wd_ref[0], preferred_element_type=F32) + bd_ref[0]
        for c in range(n_c):
            yb_ref[pl.ds(c, bm, stride=n_c), :] = y[:, c * LANES:(c + 1) * LANES]

    @pl.when(blk >= nu_ref[0])
    def _():
        yb_ref[...] = jnp.zeros_like(yb_ref)


def _experts(block_e, n_used, xs_rows, wgu_bf, bgu, wd_bf, bd, n_blocks, bm):
    n_e, d, d_ff2 = wgu_bf.shape
    d_ff = d_ff2 // 2
    grid_spec = pltpu.PrefetchScalarGridSpec(
        num_scalar_prefetch=2,
        grid=(n_blocks,),
        in_specs=[pl.BlockSpec((bm * ROW_TILE, LANES), lambda i, be, nu: (i, 0)),
                  pl.BlockSpec((1, d, d_ff2), lambda i, be, nu: (be[i], 0, 0)),
                  pl.BlockSpec((1, 1, d_ff2), lambda i, be, nu: (be[i], 0, 0)),
                  pl.BlockSpec((1, d_ff, d), lambda i, be, nu: (be[i], 0, 0)),
                  pl.BlockSpec((1, 1, d), lambda i, be, nu: (be[i], 0, 0))],
        out_specs=pl.BlockSpec((bm * ROW_TILE, LANES), lambda i, be, nu: (i, 0)))
    return pl.pallas_call(
        functools.partial(_expert_kernel, bm=bm, d=d, d_ff=d_ff),
        grid_spec=grid_spec,
        out_shape=jax.ShapeDtypeStruct((n_blocks * bm * ROW_TILE, LANES), F32),
        compiler_params=_cparams(1),
        name="experts",
    )(block_e, n_used, xs_rows, wgu_bf, bgu.reshape(n_e, 1, d_ff2), wd_bf, bd.reshape(n_e, 1, d))


def _combine_kernel(slot_ref, p_ref, xm_ref, gt_ref, gf_ref, yb_ref, y_ref, gbuf, sem, *, tc, d):
    n_c = d // LANES

    def start(t, c):
        for kk in range(TOP_K):
            _row_copy(yb_ref, slot_ref[0, kk, t], gbuf, kk * tc + t, sem).start(priority=kk % 2)
        return c

    lax.fori_loop(0, tc, start, 0)
    pltpu.make_async_copy(yb_ref.at[pl.ds(0, TOP_K * tc * ROW_TILE)], gbuf, sem).wait()
    p = p_ref[0]
    xm = xm_ref[0]
    gt2 = gt_ref[0]
    cols = []
    for c in range(n_c):
        acc = p[:, 0:1] * gbuf[pl.ds(c, tc, stride=n_c), :]
        for kk in range(1, TOP_K):
            acc = acc + p[:, kk:kk + 1] * gbuf[pl.ds(kk * tc * n_c + c, tc, stride=n_c), :]
        cols.append(acc)
    y = jnp.concatenate(cols, axis=1)
    xo = xm + gt2 * y
    y_ref[0] = _rms(xo) * gf_ref[...]


def _combine(slot_t, p_tok, xm, gt2, gfin, yb_rows):
    b, s, d = xm.shape
    tc = min(TC_COMBINE, s)
    n_i = s // tc
    kk = slot_t.shape[1]
    if gt2.shape[1] == 1:
        gt_spec = pl.BlockSpec((1, 1, d), lambda bb, ii: (bb, 0, 0))
    else:
        gt_spec = pl.BlockSpec((1, tc, d), lambda bb, ii: (bb, ii, 0))
    return pl.pallas_call(
        functools.partial(_combine_kernel, tc=tc, d=d),
        grid=(b, n_i),
        in_specs=[pl.BlockSpec((1, kk, tc), lambda bb, ii: (bb, 0, ii), memory_space=pltpu.SMEM),
                  pl.BlockSpec((1, tc, kk), lambda bb, ii: (bb, ii, 0)),
                  pl.BlockSpec((1, tc, d), lambda bb, ii: (bb, ii, 0)),
                  gt_spec,
                  pl.BlockSpec((1, d), lambda bb, ii: (0, 0)),
                  pl.BlockSpec(memory_space=pl.ANY)],
        out_specs=pl.BlockSpec((1, tc, d), lambda bb, ii: (bb, ii, 0)),
        out_shape=jax.ShapeDtypeStruct((b, s, d), F32),
        scratch_shapes=[pltpu.VMEM((TOP_K * tc * ROW_TILE, LANES), F32), pltpu.SemaphoreType.DMA(())],
        compiler_params=_cparams(2),
        name="combine",
    )(slot_t, p_tok, xm, gt2, gfin, yb_rows)


def _moe(lgt, h2_rows, xm, gt2, gfin, wgu_bf, bgu, wd_bf, bd):
    b, n_e, s = lgt.shape
    n_assign = b * s * TOP_K
    bm = BM_EXPERT if n_assign >= n_e * BM_EXPERT else BM_EXPERT_SMALL
    n_blocks = -(-n_assign // bm) + n_e
    n_slots = n_blocks * bm
    e_t, p_t, r_t, cnt = _route(lgt)
    sizes = cnt[:, 0]
    padded = ((sizes + bm - 1) // bm) * bm
    pad_end = jnp.cumsum(padded)
    pad_start = pad_end - padded
    block_start = jnp.arange(n_blocks, dtype=I32) * bm
    block_e = jnp.minimum(jnp.sum((pad_end[None, :] <= block_start[:, None]).astype(I32), axis=1), n_e - 1)
    n_used = (pad_end[-1:] // bm).astype(I32)
    slot_t = _slots(pad_start, e_t, r_t, n_e)
    xs_rows = _dispatch(pad_start, sizes, padded, n_used, slot_t, h2_rows, n_blocks, bm)
    yb_rows = _experts(block_e, n_used, xs_rows, wgu_bf, bgu, wd_bf, bd, n_blocks, bm)
    return _combine(slot_t, jnp.transpose(p_t, (0, 2, 1)), xm, gt2, gfin, yb_rows)


def kernel(x_prompt, x_sample, cache_k, cache_v, page_table, c_prompt, c_sample, w_ada, b_ada, g_norm_mix,
           g_norm_ffn, w_in, b_sb, w_s, b_s, g_sb_out, g_mlp_out, w_out, w_router, b_router, w_gate_up,
           b_gate_up, w_down, b_down, g_final):
    b, s, d = x_prompt.shape
    bd, t_new, _ = x_sample.shape
    depth, n_pool, page, h_sb, d_head = cache_k.shape
    assert depth == 1 and d_head == D_HEAD
    d_sb = h_sb * d_head
    d_mlp = g_mlp_out.shape[1]
    g_mlp = d_mlp // d_head
    n_e = w_router.shape[2]
    chunk = w_s.shape[2]
    n_tok_s = bd * t_new

    win_bf = w_in[0].astype(BF16)
    wout_bf = w_out[0].astype(BF16)
    wada_bf = w_ada[0].astype(BF16)
    wrt_bf = jnp.transpose(w_router[0]).astype(BF16)
    wgu_bf = w_gate_up[0].astype(BF16)
    wd_bf = w_down[0].astype(BF16)
    brt_b = jnp.broadcast_to(b_router[0][:, None], (n_e, LANES))
    g1 = g_norm_mix[0].reshape(1, d)
    g2 = g_norm_ffn[0].reshape(1, d)
    gsb = g_sb_out[0].reshape(1, d_sb)
    gml = g_mlp_out[0].reshape(1, d_mlp)
    gfin = g_final.reshape(1, d)
    bs_full = jnp.repeat(jnp.transpose(b_s[0]), d_head, axis=1)
    bias_rows = jnp.broadcast_to(jnp.repeat(b_sb[0], t_new)[:, None], (h_sb * t_new, LANES))

    ada = _ada(jnp.concatenate([c_prompt, c_sample], axis=0), wada_bf, b_ada[0])
    ada_p = ada[:b].reshape(b, N_ADA, d)
    ada_s = jnp.repeat(ada[b:].reshape(bd, N_ADA, d), t_new, axis=0)

    k_p, v_p, xm_p, h2_p, lg_p = _mix_prompt(x_prompt, ada_p, b_sb[0], g1, win_bf, w_s[0], bs_full, gsb, gml,
                                             wout_bf, g2, wrt_bf, brt_b, page)
    y_prompt = _moe(lg_p, h2_p, xm_p, ada_p[:, 5:6, :], gfin, wgu_bf, b_gate_up[0], wd_bf, b_down[0])

    x_s2 = x_sample.reshape(n_tok_s, d)
    q_s, k_s, v_s, u_s, gv_s = _s_pre(x_s2, ada_s[:, 0], ada_s[:, 1], g1, win_bf, d_sb, d_mlp)
    cache_kt = jnp.transpose(cache_k[0], (0, 2, 3, 1)).reshape(n_pool, d_sb, page)
    cache_vt = jnp.transpose(cache_v[0], (0, 2, 3, 1)).reshape(n_pool, d_sb, page)
    sb_s = _s_attn(q_s.reshape(bd, t_new, d_sb), k_s.reshape(bd, t_new, d_sb), v_s.reshape(bd, t_new, d_sb),
                   bias_rows, cache_kt, cache_vt, page_table)
    w_tril = jnp.tril(w_s[0][:, :t_new, :t_new])
    tt = jnp.arange(t_new)
    coef = jnp.stack([jnp.where((tt >= dd)[None, :], w_tril[:, tt, jnp.maximum(tt - dd, 0)], 0.0)
                      for dd in range(t_new)])
    coef = jnp.tile(jnp.repeat(jnp.transpose(coef, (0, 2, 1)), d_head, axis=2), (1, bd, 1))
    bias_t = jnp.tile(jnp.repeat(jnp.transpose(b_s[0][:, :t_new]), d_head, axis=1), (bd, 1))
    xm_s, h2_s, lg_s = _s_post(x_s2, sb_s.reshape(n_tok_s, d_sb), u_s, gv_s, coef, bias_t, ada_s[:, 2],
                               ada_s[:, 3], ada_s[:, 4], gsb, gml, wout_bf, g2, wrt_bf, brt_b, t_new)
    y_sample = _moe(lg_s, h2_s, xm_s.reshape(1, n_tok_s, d), ada_s[:, 5].reshape(1, n_tok_s, d), gfin,
                    wgu_bf, b_gate_up[0], wd_bf, b_down[0])

    return (y_prompt, y_sample.reshape(bd, t_new, d),
            jnp.transpose(k_p.reshape(b, s // page, h_sb, d_head, page), (0, 1, 4, 2, 3))[None],
            jnp.transpose(v_p.reshape(b, s // page, h_sb, d_head, page), (0, 1, 4, 2, 3))[None],
            k_s.reshape(1, bd, t_new, h_sb, d_head), v_s.reshape(1, bd, t_new, h_sb, d_head),
            gv_s.reshape(1, bd, t_new, d_mlp))
```

```python
import functools
import math

import jax
import jax.numpy as jnp
from jax import lax
from jax.experimental import pallas as pl
from jax.experimental.pallas import tpu as pltpu

F32 = jnp.float32
BF16 = jnp.bfloat16
I32 = jnp.int32

D_HEAD = 64
TOP_K = 4
N_ADA = 6
EPS = 1e-6
SWIGLU_LIMIT = 7.0
SWIGLU_ALPHA = 1.702
LANES = 128
SUBLANES = 8
ROW_TILE = SUBLANES
VMEM_LIMIT = 56 * 1024 * 1024

TS_MIX = 256
TT_MOE = 512
SEG_CHUNK = 32
BM_EXPERT = 512
BM_EXPERT_SMALL = 128
PAGES_PER_STEP = 16


def _cparams(n_axes):
    return pltpu.CompilerParams(dimension_semantics=("arbitrary",) * n_axes, vmem_limit_bytes=VMEM_LIMIT)


def _rms(x):
    return x * lax.rsqrt(jnp.mean(x * x, axis=-1, keepdims=True) + EPS)


def _gelu(x):
    c = math.sqrt(2.0 / math.pi)
    return x * (0.5 * (1.0 + jnp.tanh(c * (x + 0.044715 * (x * x * x)))))


def _log_sigmoid_pair(z):
    t = jnp.log(1.0 + jnp.exp(-jnp.abs(z)))
    lsig = jnp.minimum(z, 0.0) - t
    return lsig, lsig - z


def _sb_weights(z, crun, triu, mask):
    lsig, ls = _log_sigmoid_pair(z)
    if mask is not None:
        ls = jnp.where(mask, ls, 0.0)
    after = jnp.dot(ls.astype(BF16), triu, preferred_element_type=F32)
    cr = jnp.concatenate([crun] * (z.shape[1] // LANES), axis=1) if z.shape[1] > LANES else crun
    a = jnp.exp(lsig + after + cr)
    if mask is not None:
        a = jnp.where(mask, a, 0.0)
    return a, crun + jnp.sum(ls, axis=1, keepdims=True)


def _ada_kernel(c_ref, w_ref, b_ref, o_ref):
    c = c_ref[...]
    s = c * (1.0 / (1.0 + jnp.exp(-c)))
    o_ref[...] = jnp.dot(s.astype(BF16), w_ref[...], preferred_element_type=F32) + b_ref[...]


def _ada(c_all, w_ada_bf, b_ada):
    r, d = c_all.shape
    n = w_ada_bf.shape[1]
    return pl.pallas_call(
        _ada_kernel,
        grid=(n // d,),
        in_specs=[pl.BlockSpec((r, d), lambda j: (0, 0)),
                  pl.BlockSpec((d, d), lambda j: (0, j)),
                  pl.BlockSpec((1, d), lambda j: (0, j))],
        out_specs=pl.BlockSpec((r, d), lambda j: (0, j)),
        out_shape=jax.ShapeDtypeStruct((r, n), F32),
        compiler_params=_cparams(1),
        name="ada",
    )(c_all, w_ada_bf, b_ada.reshape(1, n))


def _mix_tail(x, sb, mlp, gt1, sh2, sc2, gsb, gml, wout, g2, wrt, brt, xm_ref, h2_ref, lg_ref):
    rows, d = x.shape
    o = jnp.concatenate([_rms(sb) * gsb, _rms(mlp) * gml], axis=1).astype(BF16)
    xm = x + gt1 * jnp.dot(o, wout, preferred_element_type=F32)
    xm_ref[...] = xm.reshape(xm_ref.shape)
    h2 = _rms(xm) * g2 * (1.0 + sc2) + sh2
    for c in range(d // LANES):
        h2_ref[pl.ds(c, rows, stride=d // LANES), :] = h2[:, c * LANES:(c + 1) * LANES]
    lg = lax.dot_general(wrt, h2.astype(BF16), (((1,), (1,)), ((), ())), preferred_element_type=F32)
    lg = lg + jnp.concatenate([brt] * (rows // LANES), axis=1)
    lg_ref[...] = lg.reshape(lg_ref.shape)


def _mix_prompt_kernel(bsb_ref, x_ref, ada_ref, g1_ref, win_ref, ws_ref, bsf_ref, triu_ref, gsb_ref, gml_ref,
                       wout_ref, g2_ref, wrt_ref, brt_ref,
                       k_ref, v_ref, xm_ref, h2_ref, lg_ref,
                       kt_buf, v_buf, o_acc, c_acc, *, ts, n_pairs, d_sb, d_mlp):
    i = pl.program_id(1)
    x = x_ref[0]
    ada = ada_ref[0]
    sh1, sc1, gt1, sh2, sc2 = ada[0:1], ada[1:2], ada[2:3], ada[3:4], ada[4:5]
    h = _rms(x) * g1_ref[...] * (1.0 + sc1) + sh1
    p = jnp.dot(h.astype(BF16), win_ref[...], preferred_element_type=F32)
    q = p[:, :d_sb] * (D_HEAD ** -0.5)
    k = p[:, d_sb:2 * d_sb]
    v = p[:, 2 * d_sb:3 * d_sb]
    u = _gelu(p[:, 3 * d_sb:3 * d_sb + d_mlp])
    gv = _gelu(p[:, 3 * d_sb + d_mlp:])
    kt = jnp.transpose(k)
    vt = jnp.transpose(v)
    page = k_ref.shape[3]
    for r in range(ts // page):
        k_ref[0, r] = kt[:, r * page:(r + 1) * page]
        v_ref[0, r] = vt[:, r * page:(r + 1) * page]
    kt_buf[i] = kt.astype(BF16)
    v_buf[i] = v.astype(BF16)

    lane = lax.broadcasted_iota(I32, (ts, LANES), 1)
    lo = lane < D_HEAD
    row = lax.broadcasted_iota(I32, (2 * ts, ts), 0)
    col = lax.broadcasted_iota(I32, (2 * ts, ts), 1)
    diag_mask = col < jnp.where(row >= ts, row - ts, row)
    triu = triu_ref[...]
    qs = []
    for j in range(n_pairs):
        q2 = q[:, j * LANES:(j + 1) * LANES]
        qs.append(jnp.concatenate([jnp.where(lo, q2, 0.0), jnp.where(lo, 0.0, q2)], axis=0).astype(BF16))

    def pair_unit(j, c, crun, mask):
        kt = kt_buf[c, j * LANES:(j + 1) * LANES, :]
        vv = v_buf[c, :, j * LANES:(j + 1) * LANES]
        z = jnp.dot(qs[j], kt, preferred_element_type=F32)
        z = jnp.concatenate([z[:ts] + bsb_ref[2 * j], z[ts:] + bsb_ref[2 * j + 1]], axis=0)
        a, crun = _sb_weights(z, crun, triu, mask)
        return jnp.dot(a.astype(BF16), vv, preferred_element_type=F32), crun

    for j in range(n_pairs):
        o, crun = pair_unit(j, i, jnp.zeros((2 * ts, LANES), F32), diag_mask)
        o_acc[j] = o
        c_acc[j] = crun

    def chunk_body(it, carry):
        c = i - 1 - it
        for j in range(n_pairs):
            o, crun = pair_unit(j, c, c_acc[j], None)
            o_acc[j] = o_acc[j] + o
            c_acc[j] = crun
        return carry

    lax.fori_loop(0, i, chunk_body, 0)
    sb = jnp.concatenate([jnp.where(lo, o_acc[j, :ts], o_acc[j, ts:]) for j in range(n_pairs)], axis=1)

    chunk = ws_ref.shape[1]
    tr = lax.broadcasted_iota(I32, (chunk, chunk), 0)
    tc = lax.broadcasted_iota(I32, (chunk, chunk), 1)
    tril = tc <= tr
    lo_c = lax.broadcasted_iota(I32, (chunk, LANES), 1) < D_HEAD
    ws = [jnp.where(tril, ws_ref[g], 0.0).astype(BF16) for g in range(2 * n_pairs)]
    mlp_rows = []
    for r in range(ts // chunk):
        cols = []
        for j in range(n_pairs):
            gv2 = gv[r * chunk:(r + 1) * chunk, j * LANES:(j + 1) * LANES].astype(BF16)
            ma = jnp.dot(ws[2 * j], gv2, preferred_element_type=F32)
            mb = jnp.dot(ws[2 * j + 1], gv2, preferred_element_type=F32)
            mixed = jnp.where(lo_c, ma, mb) + bsf_ref[:, j * LANES:(j + 1) * LANES]
            cols.append(u[r * chunk:(r + 1) * chunk, j * LANES:(j + 1) * LANES] * mixed)
        mlp_rows.append(jnp.concatenate(cols, axis=1))
    mlp = jnp.concatenate(mlp_rows, axis=0)

    _mix_tail(x, sb, mlp, gt1, sh2, sc2, gsb_ref[...], gml_ref[...], wout_ref[...], g2_ref[...],
              wrt_ref[...], brt_ref[...], xm_ref, h2_ref, lg_ref)


def _mix_prompt(x, ada_p, b_sb, g1, win_bf, w_s, bs_full, gsb, gml, wout_bf, g2, wrt_bf, brt_b, page):
    b, s, d = x.shape
    ts = min(TS_MIX, s)
    n_i = s // ts
    d_sb = gsb.shape[1]
    d_mlp = gml.shape[1]
    n_pairs = d_sb // LANES
    e = wrt_bf.shape[0]
    jj = lax.broadcasted_iota(I32, (ts, ts), 0)
    ss = lax.broadcasted_iota(I32, (ts, ts), 1)
    triu = (jj > ss).astype(BF16)
    const = lambda shape: pl.BlockSpec(shape, lambda bb, ii, *_: (0,) * len(shape))
    grid_spec = pltpu.PrefetchScalarGridSpec(
        num_scalar_prefetch=1,
        grid=(b, n_i),
        in_specs=[pl.BlockSpec((1, ts, d), lambda bb, ii, *_: (bb, ii, 0)),
                  pl.BlockSpec((1, N_ADA, d), lambda bb, ii, *_: (bb, 0, 0)),
                  const((1, d)), const(win_bf.shape), const(w_s.shape), const(bs_full.shape), const((ts, ts)),
                  const((1, d_sb)), const((1, d_mlp)), const(wout_bf.shape), const((1, d)),
                  const(wrt_bf.shape), const(brt_b.shape)],
        out_specs=[pl.BlockSpec((1, ts // page, d_sb, page), lambda bb, ii, *_: (bb, ii, 0, 0)),
                   pl.BlockSpec((1, ts // page, d_sb, page), lambda bb, ii, *_: (bb, ii, 0, 0)),
                   pl.BlockSpec((1, ts, d), lambda bb, ii, *_: (bb, ii, 0)),
                   pl.BlockSpec((ts * (d // LANES), LANES), lambda bb, ii, *_: (bb * n_i + ii, 0)),
                   pl.BlockSpec((1, e, ts), lambda bb, ii, *_: (bb, 0, ii))],
        scratch_shapes=[pltpu.VMEM((n_i, d_sb, ts), BF16), pltpu.VMEM((n_i, ts, d_sb), BF16),
                        pltpu.VMEM((n_pairs, 2 * ts, LANES), F32), pltpu.VMEM((n_pairs, 2 * ts, LANES), F32)])
    return pl.pallas_call(
        functools.partial(_mix_prompt_kernel, ts=ts, n_pairs=n_pairs, d_sb=d_sb, d_mlp=d_mlp),
        grid_spec=grid_spec,
        out_shape=[jax.ShapeDtypeStruct((b, s // page, d_sb, page), F32),
                   jax.ShapeDtypeStruct((b, s // page, d_sb, page), F32),
                   jax.ShapeDtypeStruct((b, s, d), F32),
                   jax.ShapeDtypeStruct((b * s * (d // LANES), LANES), F32),
                   jax.ShapeDtypeStruct((b, e, s), F32)],
        compiler_params=_cparams(2),
        name="mix_prompt",
    )(b_sb, x, ada_p, g1, win_bf, w_s, bs_full, triu, gsb, gml, wout_bf, g2, wrt_bf, brt_b)


def _s_pre_kernel(x_ref, sh_ref, sc_ref, g1_ref, win_ref, q_ref, k_ref, v_ref, u_ref, gv_ref, *, d_sb, d_mlp):
    h = _rms(x_ref[...]) * g1_ref[...] * (1.0 + sc_ref[...]) + sh_ref[...]
    p = jnp.dot(h.astype(BF16), win_ref[...], preferred_element_type=F32)
    q_ref[...] = p[:, :d_sb] * (D_HEAD ** -0.5)
    k_ref[...] = p[:, d_sb:2 * d_sb]
    v_ref[...] = p[:, 2 * d_sb:3 * d_sb]
    u_ref[...] = _gelu(p[:, 3 * d_sb:3 * d_sb + d_mlp])
    gv_ref[...] = _gelu(p[:, 3 * d_sb + d_mlp:])


def _s_pre(x2, sh1, sc1, g1, win_bf, d_sb, d_mlp):
    n, d = x2.shape
    full = lambda shape: pl.BlockSpec(shape, lambda i: (0,) * len(shape))
    return pl.pallas_call(
        functools.partial(_s_pre_kernel, d_sb=d_sb, d_mlp=d_mlp),
        grid=(1,),
        in_specs=[full((n, d)), full((n, d)), full((n, d)), full((1, d)), full(win_bf.shape)],
        out_specs=[full((n, d_sb))] * 3 + [full((n, d_mlp))] * 2,
        out_shape=[jax.ShapeDtypeStruct((n, d_sb), F32)] * 3 + [jax.ShapeDtypeStruct((n, d_mlp), F32)] * 2,
        compiler_params=_cparams(1),
        name="s_pre",
    )(x2, sh1, sc1, g1, win_bf)


def _sb_columns(z, crun, tri, mask, out_fn):
    gw = tri.shape[0]
    n_g = z.shape[1] // gw
    rows = z.shape[0]
    lsig, ls = _log_sigmoid_pair(z)
    if mask is not None:
        ls = jnp.where(mask, ls, 0.0)
    parts = [ls[:, g * gw:(g + 1) * gw] for g in range(n_g)]
    aft = jnp.dot(jnp.concatenate(parts, axis=0).astype(BF16), tri, preferred_element_type=F32)
    cols = [None] * n_g
    for g in reversed(range(n_g)):
        cols[g] = aft[g * rows:(g + 1) * rows] + jnp.concatenate([crun] * (gw // LANES), axis=1)
        crun = crun + jnp.sum(parts[g], axis=1, keepdims=True)
    a = jnp.exp(lsig + jnp.concatenate(cols, axis=1))
    if mask is not None:
        a = jnp.where(mask, a, 0.0)
    return out_fn(a.astype(BF16)), crun


def _s_attn_kernel(pt_ref, q_ref, kn_ref, vn_ref, bias_ref, tri_ref, *refs, n_pg, t_new, n_heads):
    k_refs = refs[:n_pg]
    v_refs = refs[n_pg:2 * n_pg]
    o_ref = refs[2 * n_pg]
    o_acc, c_acc = refs[2 * n_pg + 1:]
    jj = pl.program_id(1)
    rows = n_heads * t_new
    d_sb = q_ref.shape[2]
    page = k_refs[0].shape[2]
    q_rep = jnp.concatenate([q_ref[0]] * n_heads, axis=0)
    rr = lax.broadcasted_iota(I32, (rows, d_sb), 0)
    ll = lax.broadcasted_iota(I32, (rows, d_sb), 1)
    head_mask = (ll // D_HEAD) == (rr // t_new)
    qbd = jnp.where(head_mask, q_rep, 0.0).astype(BF16)
    bias = bias_ref[...]
    tri = tri_ref[...]
    nt = (((1,), (1,)), ((), ()))

    @pl.when(jj == 0)
    def _():
        pad = jnp.zeros((page - t_new, d_sb), F32)
        kb = jnp.concatenate([kn_ref[0], pad], axis=0).astype(BF16)
        vb = jnp.concatenate([vn_ref[0], pad], axis=0).astype(BF16)
        r2 = lax.broadcasted_iota(I32, (rows, page), 0)
        c2 = lax.broadcasted_iota(I32, (rows, page), 1)
        z = lax.dot_general(qbd, kb, nt, preferred_element_type=F32) + bias
        o, crun = _sb_columns(z, jnp.zeros((rows, LANES), F32), tri[:page, :page], c2 < (r2 % t_new),
                              lambda a: jnp.dot(a, vb, preferred_element_type=F32))
        o_acc[...] = o
        c_acc[...] = crun

    order = list(reversed(range(n_pg)))
    z = jnp.concatenate([jnp.dot(qbd, k_refs[pg][0].astype(BF16), preferred_element_type=F32) + bias
                         for pg in order], axis=1)

    def values(a):
        o = None
        for ci, pg in enumerate(order):
            od = lax.dot_general(a[:, ci * page:(ci + 1) * page], v_refs[pg][0].astype(BF16), nt,
                                 preferred_element_type=F32)
            o = od if o is None else o + od
        return o

    od, crun = _sb_columns(z, c_acc[...], tri, None, values)
    o = o_acc[...] + od
    o_acc[...] = o
    c_acc[...] = crun

    @pl.when(jj == pl.num_programs(1) - 1)
    def _():
        om = jnp.where(head_mask, o, 0.0)
        acc = om[0:t_new]
        for hh in range(1, n_heads):
            acc = acc + om[hh * t_new:(hh + 1) * t_new]
        o_ref[0] = acc


def _s_attn(q3, kn3, vn3, bias_rows, cache_kt, cache_vt, page_table):
    bd, t_new, d_sb = q3.shape
    n_pages = page_table.shape[1]
    page = cache_kt.shape[2]
    n_heads = d_sb // D_HEAD
    n_pg = min(PAGES_PER_STEP, n_pages)
    n_j = n_pages // n_pg
    rows = n_heads * t_new
    gw = min(2 * LANES, n_pg * page)
    jj = lax.broadcasted_iota(I32, (gw, gw), 0)
    ss = lax.broadcasted_iota(I32, (gw, gw), 1)
    tri = (jj > ss).astype(BF16)

    def page_spec(pg):
        return pl.BlockSpec((1, d_sb, page), lambda b, j, pt: (pt[b, n_pages - 1 - (j * n_pg + pg)], 0, 0))

    tok = pl.BlockSpec((1, t_new, d_sb), lambda b, j, pt: (b, 0, 0))
    grid_spec = pltpu.PrefetchScalarGridSpec(
        num_scalar_prefetch=1,
        grid=(bd, n_j),
        in_specs=[tok, tok, tok,
                  pl.BlockSpec((rows, LANES), lambda b, j, pt: (0, 0)),
                  pl.BlockSpec((gw, gw), lambda b, j, pt: (0, 0))]
                 + [page_spec(pg) for pg in range(n_pg)] * 2,
        out_specs=tok,
        scratch_shapes=[pltpu.VMEM((rows, d_sb), F32), pltpu.VMEM((rows, LANES), F32)])
    return pl.pallas_call(
        functools.partial(_s_attn_kernel, n_pg=n_pg, t_new=t_new, n_heads=n_heads),
        grid_spec=grid_spec,
        out_shape=jax.ShapeDtypeStruct((bd, t_new, d_sb), F32),
        compiler_params=_cparams(2),
        name="s_attn",
    )(page_table, q3, kn3, vn3, bias_rows, tri, *([cache_kt] * n_pg), *([cache_vt] * n_pg))


def _s_post_kernel(x_ref, sb_ref, u_ref, gv_ref, coef_ref, bt_ref, gt1_ref, sh2_ref, sc2_ref, gsb_ref, gml_ref,
                   wout_ref, g2_ref, wrt_ref, brt_ref, xm_ref, h2_ref, lg_ref, *, t_new):
    gv = gv_ref[...]
    mixed = bt_ref[...] + coef_ref[0] * gv
    for dd in range(1, t_new):
        mixed = mixed + coef_ref[dd] * pltpu.roll(gv, dd, 0)
    mlp = u_ref[...] * mixed
    _mix_tail(x_ref[...], sb_ref[...], mlp, gt1_ref[...], sh2_ref[...], sc2_ref[...], gsb_ref[...], gml_ref[...],
              wout_ref[...], g2_ref[...], wrt_ref[...], brt_ref[...], xm_ref, h2_ref, lg_ref)


def _s_post(x2, sb2, u2, gv2, coef, bias_t, gt1, sh2, sc2, gsb, gml, wout_bf, g2, wrt_bf, brt_b, t_new):
    n, d = x2.shape
    e = wrt_bf.shape[0]
    args = (x2, sb2, u2, gv2, coef, bias_t, gt1, sh2, sc2, gsb, gml, wout_bf, g2, wrt_bf, brt_b)
    full = lambda shape: pl.BlockSpec(shape, lambda i: (0,) * len(shape))
    return pl.pallas_call(
        functools.partial(_s_post_kernel, t_new=t_new),
        grid=(1,),
        in_specs=[full(a.shape) for a in args],
        out_specs=[full((n, d)), full((n * (d // LANES), LANES)), full((1, e, n))],
        out_shape=[jax.ShapeDtypeStruct((n, d), F32), jax.ShapeDtypeStruct((n * (d // LANES), LANES), F32),
                   jax.ShapeDtypeStruct((1, e, n), F32)],
        compiler_params=_cparams(1),
        name="s_post",
    )(*args)


def _route_kernel(lg_ref, tri_ref, p_ref, l_ref, n_ref, loc_ref):
    lg = lg_ref[0]
    n_e, tr = lg.shape
    eidx = lax.broadcasted_iota(I32, (n_e, tr), 0)
    work = lg
    cnt = jnp.zeros((n_e, tr), F32)
    sels, vals = [], []
    for kk in range(TOP_K):
        m = jnp.max(work, axis=0, keepdims=True)
        ek = jnp.min(jnp.where(work == m, eidx, n_e), axis=0, keepdims=True)
        sel = eidx == ek
        work = jnp.where(sel, -jnp.inf, work)
        cnt = cnt + sel.astype(F32)
        sels.append(sel)
        vals.append(m)
    ex = [jnp.exp(vv - vals[0]) for vv in vals]
    den = ex[0] + ex[1] + ex[2] + ex[3]
    for kk in range(TOP_K):
        p_ref[0, kk:kk + 1, :] = ex[kk] / den
    before = jnp.dot(cnt.astype(BF16), tri_ref[...], preferred_element_type=F32)
    n_tok = jnp.zeros((n_e, LANES), F32) + jnp.sum(cnt, axis=1, keepdims=True)
    room = jnp.floor((n_tok + (SEG_CHUNK - 1)) * (1.0 / SEG_CHUNK)) * SEG_CHUNK
    row = lax.broadcasted_iota(I32, (n_e, LANES), 0)
    incl = room
    shift = 1
    while shift < n_e:
        incl = incl + jnp.where(row >= shift, pltpu.roll(incl, shift, 0), 0.0)
        shift *= 2
    loc = incl - room
    pos = before + jnp.concatenate([loc] * (tr // LANES), axis=1)
    for kk in range(TOP_K):
        l_ref[0, kk:kk + 1, :] = jnp.sum(jnp.where(sels[kk], pos, 0.0), axis=0, keepdims=True).astype(I32)
    n_ref[0] = n_tok.astype(I32)
    loc_ref[0] = loc.astype(I32)


def _route(lgt, tt):
    b, n_e, s = lgt.shape
    n_i = s // tt
    t1 = lax.broadcasted_iota(I32, (tt, tt), 0)
    t2 = lax.broadcasted_iota(I32, (tt, tt), 1)
    tri = (t1 < t2).astype(BF16)
    tok = pl.BlockSpec((1, TOP_K, tt), lambda bb, ii: (bb, 0, ii))
    tab = pl.BlockSpec((1, n_e, LANES), lambda bb, ii: (bb * n_i + ii, 0, 0))
    return pl.pallas_call(
        _route_kernel,
        grid=(b, n_i),
        in_specs=[pl.BlockSpec((1, n_e, tt), lambda bb, ii: (bb, 0, ii)),
                  pl.BlockSpec((tt, tt), lambda bb, ii: (0, 0))],
        out_specs=[tok, tok, tab, tab],
        out_shape=[jax.ShapeDtypeStruct((b, TOP_K, s), F32), jax.ShapeDtypeStruct((b, TOP_K, s), I32),
                   jax.ShapeDtypeStruct((b * n_i, n_e, LANES), I32), jax.ShapeDtypeStruct((b * n_i, n_e, LANES), I32)],
        compiler_params=_cparams(2),
        name="route",
    )(lgt, tri)


def _rows(ref, row, n_rows):
    return ref.at[pl.ds(pl.multiple_of(row * ROW_TILE, ROW_TILE), n_rows * ROW_TILE)]


def _segment_copies(n_ref, loc_ref, gs_ref, tile, n_e, fn):
    def per_expert(ee, carry):
        idx = tile * n_e + ee
        n_chunks = lax.shift_right_logical(n_ref[idx] + (SEG_CHUNK - 1), SEG_CHUNK.bit_length() - 1)

        def per_chunk(j, c):
            fn(loc_ref[idx] + j * SEG_CHUNK, gs_ref[idx] + j * SEG_CHUNK)
            return c

        lax.fori_loop(0, n_chunks, per_chunk, 0)
        return carry

    lax.fori_loop(0, n_e, per_expert, 0)


def _dispatch_kernel(n_ref, loc_ref, gs_ref, sz_ref, pd_ref, ps_ref, nu_ref, l_ref, h_ref, xs_ref, srt, zero_ref, sem,
                     zsem, *, tt, n_e, bm, n_blocks):
    tile = pl.program_id(0)
    slot = lax.rem(tile, 2)

    def fill_zero(first_row, n_rows):
        n_chunks = lax.shift_right_logical(n_rows + (SEG_CHUNK - 1), SEG_CHUNK.bit_length() - 1)

        def chunk(j):
            return pltpu.make_async_copy(zero_ref, _rows(xs_ref, first_row + j * SEG_CHUNK, SEG_CHUNK), zsem)

        lax.fori_loop(0, n_chunks, lambda j, c: (chunk(j).start(), c)[1], 0)
        lax.fori_loop(0, n_chunks, lambda j, c: (chunk(j).wait(), c)[1], 0)

    @pl.when(tile == 0)
    def _():
        zero_ref[...] = jnp.zeros_like(zero_ref)
        srt[...] = jnp.zeros_like(srt)

        def per_expert(ee, carry):
            fill_zero(ps_ref[ee] + sz_ref[ee], pd_ref[ee] - sz_ref[ee])
            return carry

        lax.fori_loop(0, n_e, per_expert, 0)
        fill_zero(nu_ref[0] * bm, (n_blocks - nu_ref[0]) * bm)

    def scatter(t8, carry):
        base = t8 * SUBLANES
        rows = [h_ref[pl.ds(pl.multiple_of((base + u) * ROW_TILE, ROW_TILE), ROW_TILE), :] for u in range(SUBLANES)]
        for u in range(SUBLANES):
            for kk in range(TOP_K):
                dst = pl.multiple_of(l_ref[0, kk, base + u] * ROW_TILE, ROW_TILE)
                srt[slot, pl.ds(dst, ROW_TILE), :] = rows[u]
        return carry

    lax.fori_loop(0, tt // SUBLANES, scatter, 0)

    def copy(s):
        return lambda lrow, grow: pltpu.make_async_copy(_rows(srt.at[s], lrow, SEG_CHUNK),
                                                        _rows(xs_ref, grow, SEG_CHUNK), sem.at[s])

    @pl.when(tile > 0)
    def _():
        mk = copy(1 - slot)
        _segment_copies(n_ref, loc_ref, gs_ref, tile - 1, n_e, lambda lrow, grow: mk(lrow, grow).wait())

    mk = copy(slot)
    _segment_copies(n_ref, loc_ref, gs_ref, tile, n_e, lambda lrow, grow: mk(lrow, grow).start())

    @pl.when(tile == pl.num_programs(0) - 1)
    def _():
        _segment_copies(n_ref, loc_ref, gs_ref, tile, n_e, lambda lrow, grow: mk(lrow, grow).wait())


def _dispatch(tables, l_t, h2_rows, n_blocks, bm, tt):
    b, kk, s = l_t.shape
    n_i = s // tt
    n_e = tables[3].shape[0]
    srt_rows = TOP_K * tt + n_e * SEG_CHUNK
    grid_spec = pltpu.PrefetchScalarGridSpec(
        num_scalar_prefetch=7,
        grid=(b * n_i,),
        in_specs=[pl.BlockSpec((1, kk, tt), lambda t, *_: (t // n_i, 0, t % n_i), memory_space=pltpu.SMEM),
                  pl.BlockSpec((tt * ROW_TILE, LANES), lambda t, *_: (t, 0))],
        out_specs=pl.BlockSpec(memory_space=pl.ANY),
        scratch_shapes=[pltpu.VMEM((2, srt_rows * ROW_TILE, LANES), F32),
                        pltpu.VMEM((SEG_CHUNK * ROW_TILE, LANES), F32),
                        pltpu.SemaphoreType.DMA((2,)), pltpu.SemaphoreType.DMA(())])
    return pl.pallas_call(
        functools.partial(_dispatch_kernel, tt=tt, n_e=n_e, bm=bm, n_blocks=n_blocks),
        grid_spec=grid_spec,
        out_shape=jax.ShapeDtypeStruct((n_blocks * bm * ROW_TILE, LANES), F32),
        compiler_params=_cparams(1),
        name="dispatch",
    )(*tables, l_t, h2_rows)


def _expert_kernel(be_ref, nu_ref, xs_ref, wgu_ref, bgu_ref, wd_ref, bd_ref, yb_ref, *, bm, d, d_ff):
    blk = pl.program_id(0)
    n_c = d // LANES

    @pl.when(blk < nu_ref[0])
    def _():
        x = jnp.concatenate([xs_ref[pl.ds(c, bm, stride=n_c), :] for c in range(n_c)], axis=1)
        gu = jnp.dot(x.astype(BF16), wgu_ref[0], preferred_element_type=F32) + bgu_ref[0]
        gate = jnp.minimum(gu[:, :d_ff], SWIGLU_LIMIT)
        up = jnp.clip(gu[:, d_ff:], -SWIGLU_LIMIT, SWIGLU_LIMIT)
        act = gate * (1.0 / (1.0 + jnp.exp(-SWIGLU_ALPHA * gate))) * (up + 1.0)
        y = jnp.dot(act.astype(BF16), wd_ref[0], preferred_element_type=F32) + bd_ref[0]
        for c in range(n_c):
            yb_ref[pl.ds(c, bm, stride=n_c), :] = y[:, c * LANES:(c + 1) * LANES]

    @pl.when(blk >= nu_ref[0])
    def _():
        yb_ref[...] = jnp.zeros_like(yb_ref)


def _experts(block_e, n_used, xs_rows, wgu_bf, bgu, wd_bf, bd, n_blocks, bm):
    n_e, d, d_ff2 = wgu_bf.shape
    d_ff = d_ff2 // 2
    grid_spec = pltpu.PrefetchScalarGridSpec(
        num_scalar_prefetch=2,
        grid=(n_blocks,),
        in_specs=[pl.BlockSpec((bm * ROW_TILE, LANES), lambda i, be, nu: (i, 0)),
                  pl.BlockSpec((1, d, d_ff2), lambda i, be, nu: (be[i], 0, 0)),
                  pl.BlockSpec((1, 1, d_ff2), lambda i, be, nu: (be[i], 0, 0)),
                  pl.BlockSpec((1, d_ff, d), lambda i, be, nu: (be[i], 0, 0)),
                  pl.BlockSpec((1, 1, d), lambda i, be, nu: (be[i], 0, 0))],
        out_specs=pl.BlockSpec((bm * ROW_TILE, LANES), lambda i, be, nu: (i, 0)))
    return pl.pallas_call(
        functools.partial(_expert_kernel, bm=bm, d=d, d_ff=d_ff),
        grid_spec=grid_spec,
        out_shape=jax.ShapeDtypeStruct((n_blocks * bm * ROW_TILE, LANES), F32),
        compiler_params=_cparams(1),
        name="experts",
    )(block_e, n_used, xs_rows, wgu_bf, bgu.reshape(n_e, 1, d_ff2), wd_bf, bd.reshape(n_e, 1, d))


def _combine_kernel(n_ref, loc_ref, gs_ref, l_ref, p_ref, xm_ref, gt_ref, gf_ref, yb_ref, y_ref, gat, ytile, sem, *,
                    tt, n_e, d):
    tile = pl.program_id(0)
    slot = lax.rem(tile, 2)
    n_c = d // LANES

    def copy(s):
        return lambda lrow, grow: pltpu.make_async_copy(_rows(yb_ref, grow, SEG_CHUNK),
                                                        _rows(gat.at[s], lrow, SEG_CHUNK), sem.at[s])

    @pl.when(tile == 0)
    def _():
        mk = copy(slot)
        _segment_copies(n_ref, loc_ref, gs_ref, tile, n_e, lambda lrow, grow: mk(lrow, grow).start())

    @pl.when(tile + 1 < pl.num_programs(0))
    def _():
        mk = copy(1 - slot)
        _segment_copies(n_ref, loc_ref, gs_ref, tile + 1, n_e, lambda lrow, grow: mk(lrow, grow).start())

    mk = copy(slot)
    _segment_copies(n_ref, loc_ref, gs_ref, tile, n_e, lambda lrow, grow: mk(lrow, grow).wait())

    def gather(t8, carry):
        base = t8 * SUBLANES
        for u in range(SUBLANES):
            t = base + u
            acc = None
            for kk in range(TOP_K):
                src = pl.multiple_of(l_ref[0, kk, t] * ROW_TILE, ROW_TILE)
                term = p_ref[0, kk, t] * gat[slot, pl.ds(src, ROW_TILE), :]
                acc = term if acc is None else acc + term
            ytile[pl.ds(pl.multiple_of(t * ROW_TILE, ROW_TILE), ROW_TILE), :] = acc
        return carry

    lax.fori_loop(0, tt // SUBLANES, gather, 0)
    y = jnp.concatenate([ytile[pl.ds(c, tt, stride=n_c), :] for c in range(n_c)], axis=1)
    xo = xm_ref[0] + gt_ref[0] * y
    y_ref[0] = _rms(xo) * gf_ref[...]


def _combine(tables, l_t, p_t, xm, gt2, gfin, yb_rows, tt):
    b, s, d = xm.shape
    n_i = s // tt
    kk = l_t.shape[1]
    n_e = tables[0].shape[0] // (b * n_i)
    gat_rows = TOP_K * tt + n_e * SEG_CHUNK
    if gt2.shape[1] == 1:
        gt_spec = pl.BlockSpec((1, 1, d), lambda t, *_: (t // n_i, 0, 0))
    else:
        gt_spec = pl.BlockSpec((1, tt, d), lambda t, *_: (t // n_i, t % n_i, 0))
    tok_smem = pl.BlockSpec((1, kk, tt), lambda t, *_: (t // n_i, 0, t % n_i), memory_space=pltpu.SMEM)
    grid_spec = pltpu.PrefetchScalarGridSpec(
        num_scalar_prefetch=3,
        grid=(b * n_i,),
        in_specs=[tok_smem, tok_smem,
                  pl.BlockSpec((1, tt, d), lambda t, *_: (t // n_i, t % n_i, 0)),
                  gt_spec,
                  pl.BlockSpec((1, d), lambda t, *_: (0, 0)),
                  pl.BlockSpec(memory_space=pl.ANY)],
        out_specs=pl.BlockSpec((1, tt, d), lambda t, *_: (t // n_i, t % n_i, 0)),
        scratch_shapes=[pltpu.VMEM((2, gat_rows * ROW_TILE, LANES), F32), pltpu.VMEM((tt * ROW_TILE, LANES), F32),
                        pltpu.SemaphoreType.DMA((2,))])
    return pl.pallas_call(
        functools.partial(_combine_kernel, tt=tt, n_e=n_e, d=d),
        grid_spec=grid_spec,
        out_shape=jax.ShapeDtypeStruct((b, s, d), F32),
        compiler_params=_cparams(1),
        name="combine",
    )(*tables[:3], l_t, p_t, xm, gt2, gfin, yb_rows)


def _moe(lgt, h2_rows, xm, gt2, gfin, wgu_bf, bgu, wd_bf, bd):
    b, n_e, s = lgt.shape
    n_assign = b * s * TOP_K
    bm = BM_EXPERT if n_assign >= n_e * BM_EXPERT else BM_EXPERT_SMALL
    tt = min(TT_MOE, s)
    n_blocks = -(-(n_assign + n_e * (SEG_CHUNK - 1)) // bm) + n_e + 1
    p_t, l_t, n_tile, loc_tile = _route(lgt, tt)
    n2 = n_tile[:, :, 0]
    sizes = jnp.sum(n2, axis=0)
    padded = ((sizes + (SEG_CHUNK - 1) + bm - 1) // bm) * bm
    pad_end = jnp.cumsum(padded)
    pad_start = pad_end - padded
    g_start = pad_start[None, :] + jnp.cumsum(n2, axis=0) - n2
    block_start = jnp.arange(n_blocks, dtype=I32) * bm
    block_e = jnp.minimum(jnp.sum((pad_end[None, :] <= block_start[:, None]).astype(I32), axis=1), n_e - 1)
    n_used = (pad_end[-1:] // bm).astype(I32)
    tables = (n2.reshape(-1), loc_tile[:, :, 0].reshape(-1), g_start.reshape(-1).astype(I32), sizes, padded,
              pad_start, n_used)
    xs_rows = _dispatch(tables, l_t, h2_rows, n_blocks, bm, tt)
    yb_rows = _experts(block_e, n_used, xs_rows, wgu_bf, bgu, wd_bf, bd, n_blocks, bm)
    return _combine(tables, l_t, p_t, xm, gt2, gfin, yb_rows, tt)


def kernel(x_prompt, x_sample, cache_k, cache_v, page_table, c_prompt, c_sample, w_ada, b_ada, g_norm_mix,
           g_norm_ffn, w_in, b_sb, w_s, b_s, g_sb_out, g_mlp_out, w_out, w_router, b_router, w_gate_up,
           b_gate_up, w_down, b_down, g_final):
    b, s, d = x_prompt.shape
    bd, t_new, _ = x_sample.shape
    depth, n_pool, page, h_sb, d_head = cache_k.shape
    assert depth == 1 and d_head == D_HEAD
    d_sb = h_sb * d_head
    d_mlp = g_mlp_out.shape[1]
    g_mlp = d_mlp // d_head
    n_e = w_router.shape[2]
    chunk = w_s.shape[2]
    n_tok_s = bd * t_new

    win_bf = w_in[0].astype(BF16)
    wout_bf = w_out[0].astype(BF16)
    wada_bf = w_ada[0].astype(BF16)
    wrt_bf = jnp.transpose(w_router[0]).astype(BF16)
    wgu_bf = w_gate_up[0].astype(BF16)
    wd_bf = w_down[0].astype(BF16)
    brt_b = jnp.broadcast_to(b_router[0][:, None], (n_e, LANES))
    g1 = g_norm_mix[0].reshape(1, d)
    g2 = g_norm_ffn[0].reshape(1, d)
    gsb = g_sb_out[0].reshape(1, d_sb)
    gml = g_mlp_out[0].reshape(1, d_mlp)
    gfin = g_final.reshape(1, d)
    bs_full = jnp.repeat(jnp.transpose(b_s[0]), d_head, axis=1)
    bias_rows = jnp.broadcast_to(jnp.repeat(b_sb[0], t_new)[:, None], (h_sb * t_new, LANES))

    ada = _ada(jnp.concatenate([c_prompt, c_sample], axis=0), wada_bf, b_ada[0])
    ada_p = ada[:b].reshape(b, N_ADA, d)
    ada_s = jnp.repeat(ada[b:].reshape(bd, N_ADA, d), t_new, axis=0)

    k_p, v_p, xm_p, h2_p, lg_p = _mix_prompt(x_prompt, ada_p, b_sb[0], g1, win_bf, w_s[0], bs_full, gsb, gml,
                                             wout_bf, g2, wrt_bf, brt_b, page)
    y_prompt = _moe(lg_p, h2_p, xm_p, ada_p[:, 5:6, :], gfin, wgu_bf, b_gate_up[0], wd_bf, b_down[0])

    x_s2 = x_sample.reshape(n_tok_s, d)
    q_s, k_s, v_s, u_s, gv_s = _s_pre(x_s2, ada_s[:, 0], ada_s[:, 1], g1, win_bf, d_sb, d_mlp)
    cache_kt = jnp.transpose(cache_k[0], (0, 2, 3, 1)).reshape(n_pool, d_sb, page)
    cache_vt = jnp.transpose(cache_v[0], (0, 2, 3, 1)).reshape(n_pool, d_sb, page)
    sb_s = _s_attn(q_s.reshape(bd, t_new, d_sb), k_s.reshape(bd, t_new, d_sb), v_s.reshape(bd, t_new, d_sb),
                   bias_rows, cache_kt, cache_vt, page_table)
    w_tril = jnp.tril(w_s[0][:, :t_new, :t_new])
    tt = jnp.arange(t_new)
    coef = jnp.stack([jnp.where((tt >= dd)[None, :], w_tril[:, tt, jnp.maximum(tt - dd, 0)], 0.0)
                      for dd in range(t_new)])
    coef = jnp.tile(jnp.repeat(jnp.transpose(coef, (0, 2, 1)), d_head, axis=2), (1, bd, 1))
    bias_t = jnp.tile(jnp.repeat(jnp.transpose(b_s[0][:, :t_new]), d_head, axis=1), (bd, 1))
    xm_s, h2_s, lg_s = _s_post(x_s2, sb_s.reshape(n_tok_s, d_sb), u_s, gv_s, coef, bias_t, ada_s[:, 2],
                               ada_s[:, 3], ada_s[:, 4], gsb, gml, wout_bf, g2, wrt_bf, brt_b, t_new)
    y_sample = _moe(lg_s, h2_s, xm_s.reshape(1, n_tok_s, d), ada_s[:, 5].reshape(1, n_tok_s, d), gfin,
                    wgu_bf, b_gate_up[0], wd_bf, b_down[0])

    return (y_prompt, y_sample.reshape(bd, t_new, d),
            jnp.transpose(k_p.reshape(b, s // page, h_sb, d_head, page), (0, 1, 4, 2, 3))[None],
            jnp.transpose(v_p.reshape(b, s // page, h_sb, d_head, page), (0, 1, 4, 2, 3))[None],
            k_s.reshape(1, bd, t_new, h_sb, d_head), v_s.reshape(1, bd, t_new, h_sb, d_head),
            gv_s.reshape(1, bd, t_new, d_mlp))
```

```python
import functools
import math

import jax
import jax.numpy as jnp
from jax import lax
from jax.experimental import pallas as pl
from jax.experimental.pallas import tpu as pltpu

F32 = jnp.float32
BF16 = jnp.bfloat16
I32 = jnp.int32

D_HEAD = 64
TOP_K = 4
N_ADA = 6
EPS = 1e-6
SWIGLU_LIMIT = 7.0
SWIGLU_ALPHA = 1.702
LANES = 128
SUBLANES = 8
ROW_TILE = SUBLANES
VMEM_LIMIT = 56 * 1024 * 1024

TS_MIX = 256
TT_MOE = 512
SEG_CHUNK = 32
BM_EXPERT = 512
BM_EXPERT_SMALL = 128
PAGES_PER_STEP = 16


def _cparams(n_axes):
    return pltpu.CompilerParams(dimension_semantics=("arbitrary",) * n_axes, vmem_limit_bytes=VMEM_LIMIT)


def _rms(x):
    return x * lax.rsqrt(jnp.mean(x * x, axis=-1, keepdims=True) + EPS)


def _gelu(x):
    c = math.sqrt(2.0 / math.pi)
    return x * (0.5 * (1.0 + jnp.tanh(c * (x + 0.044715 * (x * x * x)))))


def _log_sigmoid_pair(z):
    t = jnp.log(1.0 + jnp.exp(-jnp.abs(z)))
    lsig = jnp.minimum(z, 0.0) - t
    return lsig, lsig - z


def _sb_weights(z, crun, triu, mask):
    lsig, ls = _log_sigmoid_pair(z)
    if mask is not None:
        ls = jnp.where(mask, ls, 0.0)
    after = jnp.dot(ls.astype(BF16), triu, preferred_element_type=F32)
    cr = jnp.concatenate([crun] * (z.shape[1] // LANES), axis=1) if z.shape[1] > LANES else crun
    a = jnp.exp(lsig + after + cr)
    if mask is not None:
        a = jnp.where(mask, a, 0.0)
    return a, crun + jnp.sum(ls, axis=1, keepdims=True)


def _ada_kernel(c_ref, w_ref, b_ref, o_ref):
    c = c_ref[...]
    s = c * (1.0 / (1.0 + jnp.exp(-c)))
    o_ref[...] = jnp.dot(s.astype(BF16), w_ref[...], preferred_element_type=F32) + b_ref[...]


def _ada(c_all, w_ada_bf, b_ada):
    r, d = c_all.shape
    n = w_ada_bf.shape[1]
    return pl.pallas_call(
        _ada_kernel,
        grid=(n // d,),
        in_specs=[pl.BlockSpec((r, d), lambda j: (0, 0)),
                  pl.BlockSpec((d, d), lambda j: (0, j)),
                  pl.BlockSpec((1, d), lambda j: (0, j))],
        out_specs=pl.BlockSpec((r, d), lambda j: (0, j)),
        out_shape=jax.ShapeDtypeStruct((r, n), F32),
        compiler_params=_cparams(1),
        name="ada",
    )(c_all, w_ada_bf, b_ada.reshape(1, n))


def _mix_tail(x, sb, mlp, gt1, sh2, sc2, gsb, gml, wout, g2, wrt, brt, xm_ref, h2_ref, lg_ref):
    rows, d = x.shape
    o = jnp.concatenate([_rms(sb) * gsb, _rms(mlp) * gml], axis=1).astype(BF16)
    xm = x + gt1 * jnp.dot(o, wout, preferred_element_type=F32)
    xm_ref[...] = xm.reshape(xm_ref.shape)
    h2 = _rms(xm) * g2 * (1.0 + sc2) + sh2
    for c in range(d // LANES):
        h2_ref[pl.ds(c, rows, stride=d // LANES), :] = h2[:, c * LANES:(c + 1) * LANES]
    lg = lax.dot_general(wrt, h2.astype(BF16), (((1,), (1,)), ((), ())), preferred_element_type=F32)
    lg = lg + jnp.concatenate([brt] * (rows // LANES), axis=1)
    lg_ref[...] = lg.reshape(lg_ref.shape)


def _mix_prompt_kernel(bsb_ref, x_ref, ada_ref, g1_ref, win_ref, ws_ref, bsf_ref, triu_ref, gsb_ref, gml_ref,
                       wout_ref, g2_ref, wrt_ref, brt_ref,
                       k_ref, v_ref, xm_ref, h2_ref, lg_ref,
                       kt_buf, v_buf, o_acc, c_acc, *, ts, n_pairs, d_sb, d_mlp):
    i = pl.program_id(1)
    x = x_ref[0]
    ada = ada_ref[0]
    sh1, sc1, gt1, sh2, sc2 = ada[0:1], ada[1:2], ada[2:3], ada[3:4], ada[4:5]
    h = _rms(x) * g1_ref[...] * (1.0 + sc1) + sh1
    p = jnp.dot(h.astype(BF16), win_ref[...], preferred_element_type=F32)
    q = p[:, :d_sb] * (D_HEAD ** -0.5)
    k = p[:, d_sb:2 * d_sb]
    v = p[:, 2 * d_sb:3 * d_sb]
    u = _gelu(p[:, 3 * d_sb:3 * d_sb + d_mlp])
    gv = _gelu(p[:, 3 * d_sb + d_mlp:])
    kt = jnp.transpose(k)
    vt = jnp.transpose(v)
    page = k_ref.shape[3]
    for r in range(ts // page):
        k_ref[0, r] = kt[:, r * page:(r + 1) * page]
        v_ref[0, r] = vt[:, r * page:(r + 1) * page]
    kt_buf[i] = kt.astype(BF16)
    v_buf[i] = v.astype(BF16)

    lane = lax.broadcasted_iota(I32, (ts, LANES), 1)
    lo = lane < D_HEAD
    row = lax.broadcasted_iota(I32, (2 * ts, ts), 0)
    col = lax.broadcasted_iota(I32, (2 * ts, ts), 1)
    diag_mask = col < jnp.where(row >= ts, row - ts, row)
    triu = triu_ref[...]
    qs = []
    for j in range(n_pairs):
        q2 = q[:, j * LANES:(j + 1) * LANES]
        qs.append(jnp.concatenate([jnp.where(lo, q2, 0.0), jnp.where(lo, 0.0, q2)], axis=0).astype(BF16))

    def pair_unit(j, c, crun, mask):
        kt = kt_buf[c, j * LANES:(j + 1) * LANES, :]
        vv = v_buf[c, :, j * LANES:(j + 1) * LANES]
        z = jnp.dot(qs[j], kt, preferred_element_type=F32)
        z = jnp.concatenate([z[:ts] + bsb_ref[2 * j], z[ts:] + bsb_ref[2 * j + 1]], axis=0)
        a, crun = _sb_weights(z, crun, triu, mask)
        return jnp.dot(a.astype(BF16), vv, preferred_element_type=F32), crun

    for j in range(n_pairs):
        o, crun = pair_unit(j, i, jnp.zeros((2 * ts, LANES), F32), diag_mask)
        o_acc[j] = o
        c_acc[j] = crun

    def chunk_body(it, carry):
        c = i - 1 - it
        for j in range(n_pairs):
            o, crun = pair_unit(j, c, c_acc[j], None)
            o_acc[j] = o_acc[j] + o
            c_acc[j] = crun
        return carry

    lax.fori_loop(0, i, chunk_body, 0)
    sb = jnp.concatenate([jnp.where(lo, o_acc[j, :ts], o_acc[j, ts:]) for j in range(n_pairs)], axis=1)

    chunk = ws_ref.shape[1]
    tr = lax.broadcasted_iota(I32, (chunk, chunk), 0)
    tc = lax.broadcasted_iota(I32, (chunk, chunk), 1)
    tril = tc <= tr
    lo_c = lax.broadcasted_iota(I32, (chunk, LANES), 1) < D_HEAD
    ws = [jnp.where(tril, ws_ref[g], 0.0).astype(BF16) for g in range(2 * n_pairs)]
    mlp_rows = []
    for r in range(ts // chunk):
        cols = []
        for j in range(n_pairs):
            gv2 = gv[r * chunk:(r + 1) * chunk, j * LANES:(j + 1) * LANES].astype(BF16)
            ma = jnp.dot(ws[2 * j], gv2, preferred_element_type=F32)
            mb = jnp.dot(ws[2 * j + 1], gv2, preferred_element_type=F32)
            mixed = jnp.where(lo_c, ma, mb) + bsf_ref[:, j * LANES:(j + 1) * LANES]
            cols.append(u[r * chunk:(r + 1) * chunk, j * LANES:(j + 1) * LANES] * mixed)
        mlp_rows.append(jnp.concatenate(cols, axis=1))
    mlp = jnp.concatenate(mlp_rows, axis=0)

    _mix_tail(x, sb, mlp, gt1, sh2, sc2, gsb_ref[...], gml_ref[...], wout_ref[...], g2_ref[...],
              wrt_ref[...], brt_ref[...], xm_ref, h2_ref, lg_ref)


def _mix_prompt(x, ada_p, b_sb, g1, win_bf, w_s, bs_full, gsb, gml, wout_bf, g2, wrt_bf, brt_b, page):
    b, s, d = x.shape
    ts = min(TS_MIX, s)
    n_i = s // ts
    d_sb = gsb.shape[1]
    d_mlp = gml.shape[1]
    n_pairs = d_sb // LANES
    e = wrt_bf.shape[0]
    jj = lax.broadcasted_iota(I32, (ts, ts), 0)
    ss = lax.broadcasted_iota(I32, (ts, ts), 1)
    triu = (jj > ss).astype(BF16)
    const = lambda shape: pl.BlockSpec(shape, lambda bb, ii, *_: (0,) * len(shape))
    grid_spec = pltpu.PrefetchScalarGridSpec(
        num_scalar_prefetch=1,
        grid=(b, n_i),
        in_specs=[pl.BlockSpec((1, ts, d), lambda bb, ii, *_: (bb, ii, 0)),
                  pl.BlockSpec((1, N_ADA, d), lambda bb, ii, *_: (bb, 0, 0)),
                  const((1, d)), const(win_bf.shape), const(w_s.shape), const(bs_full.shape), const((ts, ts)),
                  const((1, d_sb)), const((1, d_mlp)), const(wout_bf.shape), const((1, d)),
                  const(wrt_bf.shape), const(brt_b.shape)],
        out_specs=[pl.BlockSpec((1, ts // page, d_sb, page), lambda bb, ii, *_: (bb, ii, 0, 0)),
                   pl.BlockSpec((1, ts // page, d_sb, page), lambda bb, ii, *_: (bb, ii, 0, 0)),
                   pl.BlockSpec((1, ts, d), lambda bb, ii, *_: (bb, ii, 0)),
                   pl.BlockSpec((ts * (d // LANES), LANES), lambda bb, ii, *_: (bb * n_i + ii, 0)),
                   pl.BlockSpec((1, e, ts), lambda bb, ii, *_: (bb, 0, ii))],
        scratch_shapes=[pltpu.VMEM((n_i, d_sb, ts), BF16), pltpu.VMEM((n_i, ts, d_sb), BF16),
                        pltpu.VMEM((n_pairs, 2 * ts, LANES), F32), pltpu.VMEM((n_pairs, 2 * ts, LANES), F32)])
    return pl.pallas_call(
        functools.partial(_mix_prompt_kernel, ts=ts, n_pairs=n_pairs, d_sb=d_sb, d_mlp=d_mlp),
        grid_spec=grid_spec,
        out_shape=[jax.ShapeDtypeStruct((b, s // page, d_sb, page), F32),
                   jax.ShapeDtypeStruct((b, s // page, d_sb, page), F32),
                   jax.ShapeDtypeStruct((b, s, d), F32),
                   jax.ShapeDtypeStruct((b * s * (d // LANES), LANES), F32),
                   jax.ShapeDtypeStruct((b, e, s), F32)],
        compiler_params=_cparams(2),
        name="mix_prompt",
    )(b_sb, x, ada_p, g1, win_bf, w_s, bs_full, triu, gsb, gml, wout_bf, g2, wrt_bf, brt_b)


def _s_pre_kernel(x_ref, sh_ref, sc_ref, g1_ref, win_ref, q_ref, k_ref, v_ref, u_ref, gv_ref, *, d_sb, d_mlp):
    h = _rms(x_ref[...]) * g1_ref[...] * (1.0 + sc_ref[...]) + sh_ref[...]
    p = jnp.dot(h.astype(BF16), win_ref[...], preferred_element_type=F32)
    q_ref[...] = p[:, :d_sb] * (D_HEAD ** -0.5)
    k_ref[...] = p[:, d_sb:2 * d_sb]
    v_ref[...] = p[:, 2 * d_sb:3 * d_sb]
    u_ref[...] = _gelu(p[:, 3 * d_sb:3 * d_sb + d_mlp])
    gv_ref[...] = _gelu(p[:, 3 * d_sb + d_mlp:])


def _s_pre(x2, sh1, sc1, g1, win_bf, d_sb, d_mlp):
    n, d = x2.shape
    full = lambda shape: pl.BlockSpec(shape, lambda i: (0,) * len(shape))
    return pl.pallas_call(
        functools.partial(_s_pre_kernel, d_sb=d_sb, d_mlp=d_mlp),
        grid=(1,),
        in_specs=[full((n, d)), full((n, d)), full((n, d)), full((1, d)), full(win_bf.shape)],
        out_specs=[full((n, d_sb))] * 3 + [full((n, d_mlp))] * 2,
        out_shape=[jax.ShapeDtypeStruct((n, d_sb), F32)] * 3 + [jax.ShapeDtypeStruct((n, d_mlp), F32)] * 2,
        compiler_params=_cparams(1),
        name="s_pre",
    )(x2, sh1, sc1, g1, win_bf)


def _sb_columns(z, crun, tri, mask, out_fn):
    gw = tri.shape[0]
    n_g = z.shape[1] // gw
    rows = z.shape[0]
    lsig, ls = _log_sigmoid_pair(z)
    if mask is not None:
        ls = jnp.where(mask, ls, 0.0)
    parts = [ls[:, g * gw:(g + 1) * gw] for g in range(n_g)]
    aft = jnp.dot(jnp.concatenate(parts, axis=0).astype(BF16), tri, preferred_element_type=F32)
    cols = [None] * n_g
    for g in reversed(range(n_g)):
        cols[g] = aft[g * rows:(g + 1) * rows] + jnp.concatenate([crun] * (gw // LANES), axis=1)
        crun = crun + jnp.sum(parts[g], axis=1, keepdims=True)
    a = jnp.exp(lsig + jnp.concatenate(cols, axis=1))
    if mask is not None:
        a = jnp.where(mask, a, 0.0)
    return out_fn(a.astype(BF16)), crun


def _s_attn_kernel(pt_ref, q_ref, kn_ref, vn_ref, bias_ref, tri_ref, *refs, n_pg, t_new, n_heads):
    k_refs = refs[:n_pg]
    v_refs = refs[n_pg:2 * n_pg]
    o_ref = refs[2 * n_pg]
    o_acc, c_acc = refs[2 * n_pg + 1:]
    jj = pl.program_id(1)
    rows = n_heads * t_new
    d_sb = q_ref.shape[2]
    page = k_refs[0].shape[2]
    q_rep = jnp.concatenate([q_ref[0]] * n_heads, axis=0)
    rr = lax.broadcasted_iota(I32, (rows, d_sb), 0)
    ll = lax.broadcasted_iota(I32, (rows, d_sb), 1)
    head_mask = (ll // D_HEAD) == (rr // t_new)
    qbd = jnp.where(head_mask, q_rep, 0.0).astype(BF16)
    bias = bias_ref[...]
    tri = tri_ref[...]
    nt = (((1,), (1,)), ((), ()))

    @pl.when(jj == 0)
    def _():
        pad = jnp.zeros((page - t_new, d_sb), F32)
        kb = jnp.concatenate([kn_ref[0], pad], axis=0).astype(BF16)
        vb = jnp.concatenate([vn_ref[0], pad], axis=0).astype(BF16)
        r2 = lax.broadcasted_iota(I32, (rows, page), 0)
        c2 = lax.broadcasted_iota(I32, (rows, page), 1)
        z = lax.dot_general(qbd, kb, nt, preferred_element_type=F32) + bias
        o, crun = _sb_columns(z, jnp.zeros((rows, LANES), F32), tri[:page, :page], c2 < (r2 % t_new),
                              lambda a: jnp.dot(a, vb, preferred_element_type=F32))
        o_acc[...] = o
        c_acc[...] = crun

    order = list(reversed(range(n_pg)))
    z = jnp.concatenate([jnp.dot(qbd, k_refs[pg][0].astype(BF16), preferred_element_type=F32) + bias
                         for pg in order], axis=1)

    def values(a):
        o = None
        for ci, pg in enumerate(order):
            od = lax.dot_general(a[:, ci * page:(ci + 1) * page], v_refs[pg][0].astype(BF16), nt,
                                 preferred_element_type=F32)
            o = od if o is None else o + od
        return o

    od, crun = _sb_columns(z, c_acc[...], tri, None, values)
    o = o_acc[...] + od
    o_acc[...] = o
    c_acc[...] = crun

    @pl.when(jj == pl.num_programs(1) - 1)
    def _():
        om = jnp.where(head_mask, o, 0.0)
        acc = om[0:t_new]
        for hh in range(1, n_heads):
            acc = acc + om[hh * t_new:(hh + 1) * t_new]
        o_ref[0] = acc


def _s_attn(q3, kn3, vn3, bias_rows, cache_kt, cache_vt, page_table):
    bd, t_new, d_sb = q3.shape
    n_pages = page_table.shape[1]
    page = cache_kt.shape[2]
    n_heads = d_sb // D_HEAD
    n_pg = min(PAGES_PER_STEP, n_pages)
    n_j = n_pages // n_pg
    rows = n_heads * t_new
    gw = min(2 * LANES, n_pg * page)
    jj = lax.broadcasted_iota(I32, (gw, gw), 0)
    ss = lax.broadcasted_iota(I32, (gw, gw), 1)
    tri = (jj > ss).astype(BF16)

    def page_spec(pg):
        return pl.BlockSpec((1, d_sb, page), lambda b, j, pt: (pt[b, n_pages - 1 - (j * n_pg + pg)], 0, 0))

    tok = pl.BlockSpec((1, t_new, d_sb), lambda b, j, pt: (b, 0, 0))
    grid_spec = pltpu.PrefetchScalarGridSpec(
        num_scalar_prefetch=1,
        grid=(bd, n_j),
        in_specs=[tok, tok, tok,
                  pl.BlockSpec((rows, LANES), lambda b, j, pt: (0, 0)),
                  pl.BlockSpec((gw, gw), lambda b, j, pt: (0, 0))]
                 + [page_spec(pg) for pg in range(n_pg)] * 2,
        out_specs=tok,
        scratch_shapes=[pltpu.VMEM((rows, d_sb), F32), pltpu.VMEM((rows, LANES), F32)])
    return pl.pallas_call(
        functools.partial(_s_attn_kernel, n_pg=n_pg, t_new=t_new, n_heads=n_heads),
        grid_spec=grid_spec,
        out_shape=jax.ShapeDtypeStruct((bd, t_new, d_sb), F32),
        compiler_params=_cparams(2),
        name="s_attn",
    )(page_table, q3, kn3, vn3, bias_rows, tri, *([cache_kt] * n_pg), *([cache_vt] * n_pg))


def _s_post_kernel(x_ref, sb_ref, u_ref, gv_ref, coef_ref, bt_ref, gt1_ref, sh2_ref, sc2_ref, gsb_ref, gml_ref,
                   wout_ref, g2_ref, wrt_ref, brt_ref, xm_ref, h2_ref, lg_ref, *, t_new):
    gv = gv_ref[...]
    mixed = bt_ref[...] + coef_ref[0] * gv
    for dd in range(1, t_new):
        mixed = mixed + coef_ref[dd] * pltpu.roll(gv, dd, 0)
    mlp = u_ref[...] * mixed
    _mix_tail(x_ref[...], sb_ref[...], mlp, gt1_ref[...], sh2_ref[...], sc2_ref[...], gsb_ref[...], gml_ref[...],
              wout_ref[...], g2_ref[...], wrt_ref[...], brt_ref[...], xm_ref, h2_ref, lg_ref)


def _s_post(x2, sb2, u2, gv2, coef, bias_t, gt1, sh2, sc2, gsb, gml, wout_bf, g2, wrt_bf, brt_b, t_new):
    n, d = x2.shape
    e = wrt_bf.shape[0]
    args = (x2, sb2, u2, gv2, coef, bias_t, gt1, sh2, sc2, gsb, gml, wout_bf, g2, wrt_bf, brt_b)
    full = lambda shape: pl.BlockSpec(shape, lambda i: (0,) * len(shape))
    return pl.pallas_call(
        functools.partial(_s_post_kernel, t_new=t_new),
        grid=(1,),
        in_specs=[full(a.shape) for a in args],
        out_specs=[full((n, d)), full((n * (d // LANES), LANES)), full((1, e, n))],
        out_shape=[jax.ShapeDtypeStruct((n, d), F32), jax.ShapeDtypeStruct((n * (d // LANES), LANES), F32),
                   jax.ShapeDtypeStruct((1, e, n), F32)],
        compiler_params=_cparams(1),
        name="s_post",
    )(*args)


def _route_kernel(lg_ref, tri_ref, p_ref, l_ref, n_ref, loc_ref, *, buf_rows):
    lg = lg_ref[0]
    n_e, tr = lg.shape
    eidx = lax.broadcasted_iota(I32, (n_e, tr), 0)
    work = lg
    cnt = jnp.zeros((n_e, tr), F32)
    sels, vals = [], []
    for kk in range(TOP_K):
        m = jnp.max(work, axis=0, keepdims=True)
        ek = jnp.min(jnp.where(work == m, eidx, n_e), axis=0, keepdims=True)
        sel = eidx == ek
        work = jnp.where(sel, -jnp.inf, work)
        cnt = cnt + sel.astype(F32)
        sels.append(sel)
        vals.append(m)
    ex = [jnp.exp(vv - vals[0]) for vv in vals]
    den = ex[0] + ex[1] + ex[2] + ex[3]
    for kk in range(TOP_K):
        p_ref[0, kk:kk + 1, :] = ex[kk] / den
    before = jnp.dot(cnt.astype(BF16), tri_ref[...], preferred_element_type=F32)
    n_tok = jnp.zeros((n_e, LANES), F32) + jnp.sum(cnt, axis=1, keepdims=True)
    room = jnp.floor((n_tok + (SEG_CHUNK - 1)) * (1.0 / SEG_CHUNK)) * SEG_CHUNK
    row = lax.broadcasted_iota(I32, (n_e, LANES), 0)
    incl = room
    shift = 1
    while shift < n_e:
        incl = incl + jnp.where(row >= shift, pltpu.roll(incl, shift, 0), 0.0)
        shift *= 2
    loc = incl - room
    tile = pl.program_id(0) * pl.num_programs(1) + pl.program_id(1)
    slot_base = (lax.rem(tile, 2) * buf_rows).astype(F32)
    pos = (before + jnp.concatenate([loc] * (tr // LANES), axis=1) + slot_base) * ROW_TILE
    for kk in range(TOP_K):
        l_ref[0, kk:kk + 1, :] = jnp.sum(jnp.where(sels[kk], pos, 0.0), axis=0, keepdims=True).astype(I32)
    n_ref[0] = n_tok.astype(I32)
    loc_ref[0] = loc.astype(I32)


def _group_rows(tt, n_e):
    return TOP_K * tt + n_e * SEG_CHUNK


def _route(lgt, tt):
    b, n_e, s = lgt.shape
    n_i = s // tt
    t1 = lax.broadcasted_iota(I32, (tt, tt), 0)
    t2 = lax.broadcasted_iota(I32, (tt, tt), 1)
    tri = (t1 < t2).astype(BF16)
    tok = pl.BlockSpec((1, TOP_K, tt), lambda bb, ii: (bb, 0, ii))
    tab = pl.BlockSpec((1, n_e, LANES), lambda bb, ii: (bb * n_i + ii, 0, 0))
    return pl.pallas_call(
        functools.partial(_route_kernel, buf_rows=_group_rows(tt, n_e)),
        grid=(b, n_i),
        in_specs=[pl.BlockSpec((1, n_e, tt), lambda bb, ii: (bb, 0, ii)),
                  pl.BlockSpec((tt, tt), lambda bb, ii: (0, 0))],
        out_specs=[tok, tok, tab, tab],
        out_shape=[jax.ShapeDtypeStruct((b, TOP_K, s), F32), jax.ShapeDtypeStruct((b, TOP_K, s), I32),
                   jax.ShapeDtypeStruct((b * n_i, n_e, LANES), I32), jax.ShapeDtypeStruct((b * n_i, n_e, LANES), I32)],
        compiler_params=_cparams(2),
        name="route",
    )(lgt, tri)


def _rows(ref, row, n_rows):
    return ref.at[pl.ds(pl.multiple_of(row * ROW_TILE, ROW_TILE), n_rows * ROW_TILE)]


def _segment_copies(n_ref, loc_ref, gs_ref, tile, n_e, fn):
    def per_expert(ee, carry):
        idx = tile * n_e + ee
        n_chunks = lax.shift_right_logical(n_ref[idx] + (SEG_CHUNK - 1), SEG_CHUNK.bit_length() - 1)

        def per_chunk(j, c):
            fn(loc_ref[idx] + j * SEG_CHUNK, gs_ref[idx] + j * SEG_CHUNK)
            return c

        lax.fori_loop(0, n_chunks, per_chunk, 0)
        return carry

    lax.fori_loop(0, n_e, per_expert, 0)


def _dispatch_kernel(n_ref, loc_ref, gs_ref, sz_ref, pd_ref, ps_ref, nu_ref, l_ref, h_ref, xs_ref, srt, zero_ref, sem,
                     zsem, *, tt, n_e, bm, n_blocks):
    tile = pl.program_id(0)
    slot = lax.rem(tile, 2)
    buf_rows = _group_rows(tt, n_e)

    def fill_zero(first_row, n_rows):
        n_chunks = lax.shift_right_logical(n_rows + (SEG_CHUNK - 1), SEG_CHUNK.bit_length() - 1)

        def chunk(j):
            return pltpu.make_async_copy(zero_ref, _rows(xs_ref, first_row + j * SEG_CHUNK, SEG_CHUNK), zsem)

        lax.fori_loop(0, n_chunks, lambda j, c: (chunk(j).start(), c)[1], 0)
        lax.fori_loop(0, n_chunks, lambda j, c: (chunk(j).wait(), c)[1], 0)

    @pl.when(tile == 0)
    def _():
        zero_ref[...] = jnp.zeros_like(zero_ref)
        srt[...] = jnp.zeros_like(srt)

        def per_expert(ee, carry):
            fill_zero(ps_ref[ee] + sz_ref[ee], pd_ref[ee] - sz_ref[ee])
            return carry

        lax.fori_loop(0, n_e, per_expert, 0)
        fill_zero(nu_ref[0] * bm, (n_blocks - nu_ref[0]) * bm)

    def scatter(t8, carry):
        base = t8 * SUBLANES
        rows = [h_ref[pl.ds(pl.multiple_of((base + u) * ROW_TILE, ROW_TILE), ROW_TILE), :] for u in range(SUBLANES)]
        for u in range(SUBLANES):
            for kk in range(TOP_K):
                srt[pl.ds(pl.multiple_of(l_ref[0, kk, base + u], ROW_TILE), ROW_TILE), :] = rows[u]
        return carry

    lax.fori_loop(0, tt // SUBLANES, scatter, 0)

    def copy(s):
        return lambda lrow, grow: pltpu.make_async_copy(_rows(srt, s * buf_rows + lrow, SEG_CHUNK),
                                                        _rows(xs_ref, grow, SEG_CHUNK), sem.at[s])

    @pl.when(tile > 0)
    def _():
        mk = copy(1 - slot)
        _segment_copies(n_ref, loc_ref, gs_ref, tile - 1, n_e, lambda lrow, grow: mk(lrow, grow).wait())

    mk = copy(slot)
    _segment_copies(n_ref, loc_ref, gs_ref, tile, n_e, lambda lrow, grow: mk(lrow, grow).start())

    @pl.when(tile == pl.num_programs(0) - 1)
    def _():
        _segment_copies(n_ref, loc_ref, gs_ref, tile, n_e, lambda lrow, grow: mk(lrow, grow).wait())


def _dispatch(tables, l_t, h2_rows, n_blocks, bm, tt):
    b, kk, s = l_t.shape
    n_i = s // tt
    n_e = tables[3].shape[0]
    grid_spec = pltpu.PrefetchScalarGridSpec(
        num_scalar_prefetch=7,
        grid=(b * n_i,),
        in_specs=[pl.BlockSpec((1, kk, tt), lambda t, *_: (t // n_i, 0, t % n_i), memory_space=pltpu.SMEM),
                  pl.BlockSpec((tt * ROW_TILE, LANES), lambda t, *_: (t, 0))],
        out_specs=pl.BlockSpec(memory_space=pl.ANY),
        scratch_shapes=[pltpu.VMEM((2 * _group_rows(tt, n_e) * ROW_TILE, LANES), F32),
                        pltpu.VMEM((SEG_CHUNK * ROW_TILE, LANES), F32),
                        pltpu.SemaphoreType.DMA((2,)), pltpu.SemaphoreType.DMA(())])
    return pl.pallas_call(
        functools.partial(_dispatch_kernel, tt=tt, n_e=n_e, bm=bm, n_blocks=n_blocks),
        grid_spec=grid_spec,
        out_shape=jax.ShapeDtypeStruct((n_blocks * bm * ROW_TILE, LANES), F32),
        compiler_params=_cparams(1),
        name="dispatch",
    )(*tables, l_t, h2_rows)


def _expert_kernel(be_ref, nu_ref, xs_ref, wgu_ref, bgu_ref, wd_ref, bd_ref, yb_ref, *, bm, d, d_ff):
    blk = pl.program_id(0)
    n_c = d // LANES

    @pl.when(blk < nu_ref[0])
    def _():
        x = jnp.concatenate([xs_ref[pl.ds(c, bm, stride=n_c), :] for c in range(n_c)], axis=1)
        gu = jnp.dot(x.astype(BF16), wgu_ref[0], preferred_element_type=F32) + bgu_ref[0]
        gate = jnp.minimum(gu[:, :d_ff], SWIGLU_LIMIT)
        up = jnp.clip(gu[:, d_ff:], -SWIGLU_LIMIT, SWIGLU_LIMIT)
        act = gate * (1.0 / (1.0 + jnp.exp(-SWIGLU_ALPHA * gate))) * (up + 1.0)
        y = jnp.dot(act.astype(BF16), wd_ref[0], preferred_element_type=F32) + bd_ref[0]
        for c in range(n_c):
            yb_ref[pl.ds(c, bm, stride=n_c), :] = y[:, c * LANES:(c + 1) * LANES]

    @pl.when(blk >= nu_ref[0])
    def _():
        yb_ref[...] = jnp.zeros_like(yb_ref)


def _experts(block_e, n_used, xs_rows, wgu_bf, bgu, wd_bf, bd, n_blocks, bm):
    n_e, d, d_ff2 = wgu_bf.shape
    d_ff = d_ff2 // 2
    grid_spec = pltpu.PrefetchScalarGridSpec(
        num_scalar_prefetch=2,
        grid=(n_blocks,),
        in_specs=[pl.BlockSpec((bm * ROW_TILE, LANES), lambda i, be, nu: (i, 0)),
                  pl.BlockSpec((1, d, d_ff2), lambda i, be, nu: (be[i], 0, 0)),
                  pl.BlockSpec((1, 1, d_ff2), lambda i, be, nu: (be[i], 0, 0)),
                  pl.BlockSpec((1, d_ff, d), lambda i, be, nu: (be[i], 0, 0)),
                  pl.BlockSpec((1, 1, d), lambda i, be, nu: (be[i], 0, 0))],
        out_specs=pl.BlockSpec((bm * ROW_TILE, LANES), lambda i, be, nu: (i, 0)))
    return pl.pallas_call(
        functools.partial(_expert_kernel, bm=bm, d=d, d_ff=d_ff),
        grid_spec=grid_spec,
        out_shape=jax.ShapeDtypeStruct((n_blocks * bm * ROW_TILE, LANES), F32),
        compiler_params=_cparams(1),
        name="experts",
    )(block_e, n_used, xs_rows, wgu_bf, bgu.reshape(n_e, 1, d_ff2), wd_bf, bd.reshape(n_e, 1, d))


def _combine_kernel(n_ref, loc_ref, gs_ref, l_ref, p_ref, xm_ref, gt_ref, gf_ref, yb_ref, y_ref, gat, ytile, sem, *,
                    tt, n_e, d):
    tile = pl.program_id(0)
    slot = lax.rem(tile, 2)
    buf_rows = _group_rows(tt, n_e)
    n_c = d // LANES

    def copy(s):
        return lambda lrow, grow: pltpu.make_async_copy(_rows(yb_ref, grow, SEG_CHUNK),
                                                        _rows(gat, s * buf_rows + lrow, SEG_CHUNK), sem.at[s])

    @pl.when(tile == 0)
    def _():
        mk = copy(slot)
        _segment_copies(n_ref, loc_ref, gs_ref, tile, n_e, lambda lrow, grow: mk(lrow, grow).start())

    @pl.when(tile + 1 < pl.num_programs(0))
    def _():
        mk = copy(1 - slot)
        _segment_copies(n_ref, loc_ref, gs_ref, tile + 1, n_e, lambda lrow, grow: mk(lrow, grow).start())

    mk = copy(slot)
    _segment_copies(n_ref, loc_ref, gs_ref, tile, n_e, lambda lrow, grow: mk(lrow, grow).wait())

    def gather(t8, carry):
        base = t8 * SUBLANES
        for u in range(SUBLANES):
            t = base + u
            acc = None
            for kk in range(TOP_K):
                term = p_ref[0, kk, t] * gat[pl.ds(pl.multiple_of(l_ref[0, kk, t], ROW_TILE), ROW_TILE), :]
                acc = term if acc is None else acc + term
            ytile[pl.ds(pl.multiple_of(t * ROW_TILE, ROW_TILE), ROW_TILE), :] = acc
        return carry

    lax.fori_loop(0, tt // SUBLANES, gather, 0)
    y = jnp.concatenate([ytile[pl.ds(c, tt, stride=n_c), :] for c in range(n_c)], axis=1)
    xo = xm_ref[0] + gt_ref[0] * y
    y_ref[0] = _rms(xo) * gf_ref[...]


def _combine(tables, l_t, p_t, xm, gt2, gfin, yb_rows, tt):
    b, s, d = xm.shape
    n_i = s // tt
    kk = l_t.shape[1]
    n_e = tables[0].shape[0] // (b * n_i)
    if gt2.shape[1] == 1:
        gt_spec = pl.BlockSpec((1, 1, d), lambda t, *_: (t // n_i, 0, 0))
    else:
        gt_spec = pl.BlockSpec((1, tt, d), lambda t, *_: (t // n_i, t % n_i, 0))
    tok_smem = pl.BlockSpec((1, kk, tt), lambda t, *_: (t // n_i, 0, t % n_i), memory_space=pltpu.SMEM)
    grid_spec = pltpu.PrefetchScalarGridSpec(
        num_scalar_prefetch=3,
        grid=(b * n_i,),
        in_specs=[tok_smem, tok_smem,
                  pl.BlockSpec((1, tt, d), lambda t, *_: (t // n_i, t % n_i, 0)),
                  gt_spec,
                  pl.BlockSpec((1, d), lambda t, *_: (0, 0)),
                  pl.BlockSpec(memory_space=pl.ANY)],
        out_specs=pl.BlockSpec((1, tt, d), lambda t, *_: (t // n_i, t % n_i, 0)),
        scratch_shapes=[pltpu.VMEM((2 * _group_rows(tt, n_e) * ROW_TILE, LANES), F32),
                        pltpu.VMEM((tt * ROW_TILE, LANES), F32),
                        pltpu.SemaphoreType.DMA((2,))])
    return pl.pallas_call(
        functools.partial(_combine_kernel, tt=tt, n_e=n_e, d=d),
        grid_spec=grid_spec,
        out_shape=jax.ShapeDtypeStruct((b, s, d), F32),
        compiler_params=_cparams(1),
        name="combine",
    )(*tables[:3], l_t, p_t, xm, gt2, gfin, yb_rows)


def _moe(lgt, h2_rows, xm, gt2, gfin, wgu_bf, bgu, wd_bf, bd):
    b, n_e, s = lgt.shape
    n_assign = b * s * TOP_K
    bm = BM_EXPERT if n_assign >= n_e * BM_EXPERT else BM_EXPERT_SMALL
    tt = min(TT_MOE, s)
    n_blocks = -(-(n_assign + n_e * (SEG_CHUNK - 1)) // bm) + n_e + 1
    p_t, l_t, n_tile, loc_tile = _route(lgt, tt)
    n2 = n_tile[:, :, 0]
    sizes = jnp.sum(n2, axis=0)
    padded = ((sizes + (SEG_CHUNK - 1) + bm - 1) // bm) * bm
    pad_end = jnp.cumsum(padded)
    pad_start = pad_end - padded
    g_start = pad_start[None, :] + jnp.cumsum(n2, axis=0) - n2
    block_start = jnp.arange(n_blocks, dtype=I32) * bm
    block_e = jnp.minimum(jnp.sum((pad_end[None, :] <= block_start[:, None]).astype(I32), axis=1), n_e - 1)
    n_used = (pad_end[-1:] // bm).astype(I32)
    tables = (n2.reshape(-1), loc_tile[:, :, 0].reshape(-1), g_start.reshape(-1).astype(I32), sizes, padded,
              pad_start, n_used)
    xs_rows = _dispatch(tables, l_t, h2_rows, n_blocks, bm, tt)
    yb_rows = _experts(block_e, n_used, xs_rows, wgu_bf, bgu, wd_bf, bd, n_blocks, bm)
    return _combine(tables, l_t, p_t, xm, gt2, gfin, yb_rows, tt)


def kernel(x_prompt, x_sample, cache_k, cache_v, page_table, c_prompt, c_sample, w_ada, b_ada, g_norm_mix,
           g_norm_ffn, w_in, b_sb, w_s, b_s, g_sb_out, g_mlp_out, w_out, w_router, b_router, w_gate_up,
           b_gate_up, w_down, b_down, g_final):
    b, s, d = x_prompt.shape
    bd, t_new, _ = x_sample.shape
    depth, n_pool, page, h_sb, d_head = cache_k.shape
    assert depth == 1 and d_head == D_HEAD
    d_sb = h_sb * d_head
    d_mlp = g_mlp_out.shape[1]
    g_mlp = d_mlp // d_head
    n_e = w_router.shape[2]
    chunk = w_s.shape[2]
    n_tok_s = bd * t_new

    win_bf = w_in[0].astype(BF16)
    wout_bf = w_out[0].astype(BF16)
    wada_bf = w_ada[0].astype(BF16)
    wrt_bf = jnp.transpose(w_router[0]).astype(BF16)
    wgu_bf = w_gate_up[0].astype(BF16)
    wd_bf = w_down[0].astype(BF16)
    brt_b = jnp.broadcast_to(b_router[0][:, None], (n_e, LANES))
    g1 = g_norm_mix[0].reshape(1, d)
    g2 = g_norm_ffn[0].reshape(1, d)
    gsb = g_sb_out[0].reshape(1, d_sb)
    gml = g_mlp_out[0].reshape(1, d_mlp)
    gfin = g_final.reshape(1, d)
    bs_full = jnp.repeat(jnp.transpose(b_s[0]), d_head, axis=1)
    bias_rows = jnp.broadcast_to(jnp.repeat(b_sb[0], t_new)[:, None], (h_sb * t_new, LANES))

    ada = _ada(jnp.concatenate([c_prompt, c_sample], axis=0), wada_bf, b_ada[0])
    ada_p = ada[:b].reshape(b, N_ADA, d)
    ada_s = jnp.repeat(ada[b:].reshape(bd, N_ADA, d), t_new, axis=0)

    k_p, v_p, xm_p, h2_p, lg_p = _mix_prompt(x_prompt, ada_p, b_sb[0], g1, win_bf, w_s[0], bs_full, gsb, gml,
                                             wout_bf, g2, wrt_bf, brt_b, page)
    y_prompt = _moe(lg_p, h2_p, xm_p, ada_p[:, 5:6, :], gfin, wgu_bf, b_gate_up[0], wd_bf, b_down[0])

    x_s2 = x_sample.reshape(n_tok_s, d)
    q_s, k_s, v_s, u_s, gv_s = _s_pre(x_s2, ada_s[:, 0], ada_s[:, 1], g1, win_bf, d_sb, d_mlp)
    cache_kt = jnp.transpose(cache_k[0], (0, 2, 3, 1)).reshape(n_pool, d_sb, page)
    cache_vt = jnp.transpose(cache_v[0], (0, 2, 3, 1)).reshape(n_pool, d_sb, page)
    sb_s = _s_attn(q_s.reshape(bd, t_new, d_sb), k_s.reshape(bd, t_new, d_sb), v_s.reshape(bd, t_new, d_sb),
                   bias_rows, cache_kt, cache_vt, page_table)
    w_tril = jnp.tril(w_s[0][:, :t_new, :t_new])
    tt = jnp.arange(t_new)
    coef = jnp.stack([jnp.where((tt >= dd)[None, :], w_tril[:, tt, jnp.maximum(tt - dd, 0)], 0.0)
                      for dd in range(t_new)])
    coef = jnp.tile(jnp.repeat(jnp.transpose(coef, (0, 2, 1)), d_head, axis=2), (1, bd, 1))
    bias_t = jnp.tile(jnp.repeat(jnp.transpose(b_s[0][:, :t_new]), d_head, axis=1), (bd, 1))
    xm_s, h2_s, lg_s = _s_post(x_s2, sb_s.reshape(n_tok_s, d_sb), u_s, gv_s, coef, bias_t, ada_s[:, 2],
                               ada_s[:, 3], ada_s[:, 4], gsb, gml, wout_bf, g2, wrt_bf, brt_b, t_new)
    y_sample = _moe(lg_s, h2_s, xm_s.reshape(1, n_tok_s, d), ada_s[:, 5].reshape(1, n_tok_s, d), gfin,
                    wgu_bf, b_gate_up[0], wd_bf, b_down[0])

    return (y_prompt, y_sample.reshape(bd, t_new, d),
            jnp.transpose(k_p.reshape(b, s // page, h_sb, d_head, page), (0, 1, 4, 2, 3))[None],
            jnp.transpose(v_p.reshape(b, s // page, h_sb, d_head, page), (0, 1, 4, 2, 3))[None],
            k_s.reshape(1, bd, t_new, h_sb, d_head), v_s.reshape(1, bd, t_new, h_sb, d_head),
            gv_s.reshape(1, bd, t_new, d_mlp))
```

```python
import functools
import math

import jax
import jax.numpy as jnp
from jax import lax
from jax.experimental import pallas as pl
from jax.experimental.pallas import tpu as pltpu

F32 = jnp.float32
BF16 = jnp.bfloat16
I32 = jnp.int32

D_HEAD = 64
TOP_K = 4
N_ADA = 6
EPS = 1e-6
SWIGLU_LIMIT = 7.0
SWIGLU_ALPHA = 1.702
LANES = 128
SUBLANES = 8
ROW_TILE = SUBLANES
VMEM_LIMIT = 56 * 1024 * 1024

TS_MIX = 256
TT_MOE = 512
SEG_CHUNK = 32
BM_EXPERT = 512
BM_EXPERT_SMALL = 128
PAGES_PER_STEP = 16


def _cparams(n_axes):
    return pltpu.CompilerParams(dimension_semantics=("arbitrary",) * n_axes, vmem_limit_bytes=VMEM_LIMIT)


def _rms(x):
    return x * lax.rsqrt(jnp.mean(x * x, axis=-1, keepdims=True) + EPS)


def _gelu(x):
    c = math.sqrt(2.0 / math.pi)
    return x * (0.5 * (1.0 + jnp.tanh(c * (x + 0.044715 * (x * x * x)))))


def _log_sigmoid_pair(z):
    t = jnp.log(1.0 + jnp.exp(-jnp.abs(z)))
    lsig = jnp.minimum(z, 0.0) - t
    return lsig, lsig - z


def _sb_weights(z, crun, triu, mask):
    lsig, ls = _log_sigmoid_pair(z)
    if mask is not None:
        ls = jnp.where(mask, ls, 0.0)
    after = jnp.dot(ls.astype(BF16), triu, preferred_element_type=F32)
    cr = jnp.concatenate([crun] * (z.shape[1] // LANES), axis=1) if z.shape[1] > LANES else crun
    a = jnp.exp(lsig + after + cr)
    if mask is not None:
        a = jnp.where(mask, a, 0.0)
    return a, crun + jnp.sum(ls, axis=1, keepdims=True)


def _ada_kernel(c_ref, w_ref, b_ref, o_ref):
    c = c_ref[...]
    s = c * (1.0 / (1.0 + jnp.exp(-c)))
    o_ref[...] = jnp.dot(s.astype(BF16), w_ref[...], preferred_element_type=F32) + b_ref[...]


def _ada(c_all, w_ada_bf, b_ada):
    r, d = c_all.shape
    n = w_ada_bf.shape[1]
    return pl.pallas_call(
        _ada_kernel,
        grid=(n // d,),
        in_specs=[pl.BlockSpec((r, d), lambda j: (0, 0)),
                  pl.BlockSpec((d, d), lambda j: (0, j)),
                  pl.BlockSpec((1, d), lambda j: (0, j))],
        out_specs=pl.BlockSpec((r, d), lambda j: (0, j)),
        out_shape=jax.ShapeDtypeStruct((r, n), F32),
        compiler_params=_cparams(1),
        name="ada",
    )(c_all, w_ada_bf, b_ada.reshape(1, n))


def _mix_tail(x, sb, mlp, gt1, sh2, sc2, gsb, gml, wout, g2, wrt, brt, xm_ref, h2_ref, lg_ref):
    rows, d = x.shape
    o = jnp.concatenate([_rms(sb) * gsb, _rms(mlp) * gml], axis=1).astype(BF16)
    xm = x + gt1 * jnp.dot(o, wout, preferred_element_type=F32)
    xm_ref[...] = xm.reshape(xm_ref.shape)
    h2 = _rms(xm) * g2 * (1.0 + sc2) + sh2
    for c in range(d // LANES):
        h2_ref[pl.ds(c, rows, stride=d // LANES), :] = h2[:, c * LANES:(c + 1) * LANES]
    lg = lax.dot_general(wrt, h2.astype(BF16), (((1,), (1,)), ((), ())), preferred_element_type=F32)
    lg = lg + jnp.concatenate([brt] * (rows // LANES), axis=1)
    lg_ref[...] = lg.reshape(lg_ref.shape)


def _mix_prompt_kernel(bsb_ref, x_ref, ada_ref, g1_ref, win_ref, ws_ref, bsf_ref, triu_ref, gsb_ref, gml_ref,
                       wout_ref, g2_ref, wrt_ref, brt_ref,
                       k_ref, v_ref, xm_ref, h2_ref, lg_ref,
                       kt_buf, v_buf, o_acc, c_acc, *, ts, n_pairs, d_sb, d_mlp):
    i = pl.program_id(1)
    x = x_ref[0]
    ada = ada_ref[0]
    sh1, sc1, gt1, sh2, sc2 = ada[0:1], ada[1:2], ada[2:3], ada[3:4], ada[4:5]
    h = _rms(x) * g1_ref[...] * (1.0 + sc1) + sh1
    p = jnp.dot(h.astype(BF16), win_ref[...], preferred_element_type=F32)
    q = p[:, :d_sb] * (D_HEAD ** -0.5)
    k = p[:, d_sb:2 * d_sb]
    v = p[:, 2 * d_sb:3 * d_sb]
    u = _gelu(p[:, 3 * d_sb:3 * d_sb + d_mlp])
    gv = _gelu(p[:, 3 * d_sb + d_mlp:])
    kt = jnp.transpose(k)
    vt = jnp.transpose(v)
    page = k_ref.shape[3]
    for r in range(ts // page):
        k_ref[0, r] = kt[:, r * page:(r + 1) * page]
        v_ref[0, r] = vt[:, r * page:(r + 1) * page]
    kt_buf[i] = kt.astype(BF16)
    v_buf[i] = v.astype(BF16)

    lane = lax.broadcasted_iota(I32, (ts, LANES), 1)
    lo = lane < D_HEAD
    row = lax.broadcasted_iota(I32, (2 * ts, ts), 0)
    col = lax.broadcasted_iota(I32, (2 * ts, ts), 1)
    diag_mask = col < jnp.where(row >= ts, row - ts, row)
    triu = triu_ref[...]
    qs = []
    for j in range(n_pairs):
        q2 = q[:, j * LANES:(j + 1) * LANES]
        qs.append(jnp.concatenate([jnp.where(lo, q2, 0.0), jnp.where(lo, 0.0, q2)], axis=0).astype(BF16))

    def pair_unit(j, c, crun, mask):
        kt = kt_buf[c, j * LANES:(j + 1) * LANES, :]
        vv = v_buf[c, :, j * LANES:(j + 1) * LANES]
        z = jnp.dot(qs[j], kt, preferred_element_type=F32)
        z = jnp.concatenate([z[:ts] + bsb_ref[2 * j], z[ts:] + bsb_ref[2 * j + 1]], axis=0)
        a, crun = _sb_weights(z, crun, triu, mask)
        return jnp.dot(a.astype(BF16), vv, preferred_element_type=F32), crun

    for j in range(n_pairs):
        o, crun = pair_unit(j, i, jnp.zeros((2 * ts, LANES), F32), diag_mask)
        o_acc[j] = o
        c_acc[j] = crun

    def chunk_body(it, carry):
        c = i - 1 - it
        for j in range(n_pairs):
            o, crun = pair_unit(j, c, c_acc[j], None)
            o_acc[j] = o_acc[j] + o
            c_acc[j] = crun
        return carry

    lax.fori_loop(0, i, chunk_body, 0)
    sb = jnp.concatenate([jnp.where(lo, o_acc[j, :ts], o_acc[j, ts:]) for j in range(n_pairs)], axis=1)

    chunk = ws_ref.shape[1]
    tr = lax.broadcasted_iota(I32, (chunk, chunk), 0)
    tc = lax.broadcasted_iota(I32, (chunk, chunk), 1)
    tril = tc <= tr
    lo_c = lax.broadcasted_iota(I32, (chunk, LANES), 1) < D_HEAD
    ws = [jnp.where(tril, ws_ref[g], 0.0).astype(BF16) for g in range(2 * n_pairs)]
    mlp_rows = []
    for r in range(ts // chunk):
        cols = []
        for j in range(n_pairs):
            gv2 = gv[r * chunk:(r + 1) * chunk, j * LANES:(j + 1) * LANES].astype(BF16)
            ma = jnp.dot(ws[2 * j], gv2, preferred_element_type=F32)
            mb = jnp.dot(ws[2 * j + 1], gv2, preferred_element_type=F32)
            mixed = jnp.where(lo_c, ma, mb) + bsf_ref[:, j * LANES:(j + 1) * LANES]
            cols.append(u[r * chunk:(r + 1) * chunk, j * LANES:(j + 1) * LANES] * mixed)
        mlp_rows.append(jnp.concatenate(cols, axis=1))
    mlp = jnp.concatenate(mlp_rows, axis=0)

    _mix_tail(x, sb, mlp, gt1, sh2, sc2, gsb_ref[...], gml_ref[...], wout_ref[...], g2_ref[...],
              wrt_ref[...], brt_ref[...], xm_ref, h2_ref, lg_ref)


def _mix_prompt(x, ada_p, b_sb, g1, win_bf, w_s, bs_full, gsb, gml, wout_bf, g2, wrt_bf, brt_b, page):
    b, s, d = x.shape
    ts = min(TS_MIX, s)
    n_i = s // ts
    d_sb = gsb.shape[1]
    d_mlp = gml.shape[1]
    n_pairs = d_sb // LANES
    e = wrt_bf.shape[0]
    jj = lax.broadcasted_iota(I32, (ts, ts), 0)
    ss = lax.broadcasted_iota(I32, (ts, ts), 1)
    triu = (jj > ss).astype(BF16)
    const = lambda shape: pl.BlockSpec(shape, lambda bb, ii, *_: (0,) * len(shape))
    grid_spec = pltpu.PrefetchScalarGridSpec(
        num_scalar_prefetch=1,
        grid=(b, n_i),
        in_specs=[pl.BlockSpec((1, ts, d), lambda bb, ii, *_: (bb, ii, 0)),
                  pl.BlockSpec((1, N_ADA, d), lambda bb, ii, *_: (bb, 0, 0)),
                  const((1, d)), const(win_bf.shape), const(w_s.shape), const(bs_full.shape), const((ts, ts)),
                  const((1, d_sb)), const((1, d_mlp)), const(wout_bf.shape), const((1, d)),
                  const(wrt_bf.shape), const(brt_b.shape)],
        out_specs=[pl.BlockSpec((1, ts // page, d_sb, page), lambda bb, ii, *_: (bb, ii, 0, 0)),
                   pl.BlockSpec((1, ts // page, d_sb, page), lambda bb, ii, *_: (bb, ii, 0, 0)),
                   pl.BlockSpec((1, ts, d), lambda bb, ii, *_: (bb, ii, 0)),
                   pl.BlockSpec((ts * (d // LANES), LANES), lambda bb, ii, *_: (bb * n_i + ii, 0)),
                   pl.BlockSpec((1, e, ts), lambda bb, ii, *_: (bb, 0, ii))],
        scratch_shapes=[pltpu.VMEM((n_i, d_sb, ts), BF16), pltpu.VMEM((n_i, ts, d_sb), BF16),
                        pltpu.VMEM((n_pairs, 2 * ts, LANES), F32), pltpu.VMEM((n_pairs, 2 * ts, LANES), F32)])
    return pl.pallas_call(
        functools.partial(_mix_prompt_kernel, ts=ts, n_pairs=n_pairs, d_sb=d_sb, d_mlp=d_mlp),
        grid_spec=grid_spec,
        out_shape=[jax.ShapeDtypeStruct((b, s // page, d_sb, page), F32),
                   jax.ShapeDtypeStruct((b, s // page, d_sb, page), F32),
                   jax.ShapeDtypeStruct((b, s, d), F32),
                   jax.ShapeDtypeStruct((b * s * (d // LANES), LANES), F32),
                   jax.ShapeDtypeStruct((b, e, s), F32)],
        compiler_params=_cparams(2),
        name="mix_prompt",
    )(b_sb, x, ada_p, g1, win_bf, w_s, bs_full, triu, gsb, gml, wout_bf, g2, wrt_bf, brt_b)


def _s_pre_kernel(x_ref, sh_ref, sc_ref, g1_ref, win_ref, q_ref, k_ref, v_ref, u_ref, gv_ref, *, d_sb, d_mlp):
    h = _rms(x_ref[...]) * g1_ref[...] * (1.0 + sc_ref[...]) + sh_ref[...]
    p = jnp.dot(h.astype(BF16), win_ref[...], preferred_element_type=F32)
    q_ref[...] = p[:, :d_sb] * (D_HEAD ** -0.5)
    k_ref[...] = p[:, d_sb:2 * d_sb]
    v_ref[...] = p[:, 2 * d_sb:3 * d_sb]
    u_ref[...] = _gelu(p[:, 3 * d_sb:3 * d_sb + d_mlp])
    gv_ref[...] = _gelu(p[:, 3 * d_sb + d_mlp:])


def _s_pre(x2, sh1, sc1, g1, win_bf, d_sb, d_mlp):
    n, d = x2.shape
    full = lambda shape: pl.BlockSpec(shape, lambda i: (0,) * len(shape))
    return pl.pallas_call(
        functools.partial(_s_pre_kernel, d_sb=d_sb, d_mlp=d_mlp),
        grid=(1,),
        in_specs=[full((n, d)), full((n, d)), full((n, d)), full((1, d)), full(win_bf.shape)],
        out_specs=[full((n, d_sb))] * 3 + [full((n, d_mlp))] * 2,
        out_shape=[jax.ShapeDtypeStruct((n, d_sb), F32)] * 3 + [jax.ShapeDtypeStruct((n, d_mlp), F32)] * 2,
        compiler_params=_cparams(1),
        name="s_pre",
    )(x2, sh1, sc1, g1, win_bf)


def _sb_columns(z, crun, tri, mask, out_fn):
    gw = tri.shape[0]
    n_g = z.shape[1] // gw
    rows = z.shape[0]
    lsig, ls = _log_sigmoid_pair(z)
    if mask is not None:
        ls = jnp.where(mask, ls, 0.0)
    parts = [ls[:, g * gw:(g + 1) * gw] for g in range(n_g)]
    aft = jnp.dot(jnp.concatenate(parts, axis=0).astype(BF16), tri, preferred_element_type=F32)
    cols = [None] * n_g
    for g in reversed(range(n_g)):
        cols[g] = aft[g * rows:(g + 1) * rows] + jnp.concatenate([crun] * (gw // LANES), axis=1)
        crun = crun + jnp.sum(parts[g], axis=1, keepdims=True)
    a = jnp.exp(lsig + jnp.concatenate(cols, axis=1))
    if mask is not None:
        a = jnp.where(mask, a, 0.0)
    return out_fn(a.astype(BF16)), crun


def _s_attn_kernel(pt_ref, q_ref, kn_ref, vn_ref, bias_ref, tri_ref, *refs, n_pg, t_new, n_heads):
    k_refs = refs[:n_pg]
    v_refs = refs[n_pg:2 * n_pg]
    o_ref = refs[2 * n_pg]
    o_acc, c_acc = refs[2 * n_pg + 1:]
    jj = pl.program_id(1)
    rows = n_heads * t_new
    d_sb = q_ref.shape[2]
    page = k_refs[0].shape[2]
    q_rep = jnp.concatenate([q_ref[0]] * n_heads, axis=0)
    rr = lax.broadcasted_iota(I32, (rows, d_sb), 0)
    ll = lax.broadcasted_iota(I32, (rows, d_sb), 1)
    head_mask = (ll // D_HEAD) == (rr // t_new)
    qbd = jnp.where(head_mask, q_rep, 0.0).astype(BF16)
    bias = bias_ref[...]
    tri = tri_ref[...]
    nt = (((1,), (1,)), ((), ()))

    @pl.when(jj == 0)
    def _():
        pad = jnp.zeros((page - t_new, d_sb), F32)
        kb = jnp.concatenate([kn_ref[0], pad], axis=0).astype(BF16)
        vb = jnp.concatenate([vn_ref[0], pad], axis=0).astype(BF16)
        r2 = lax.broadcasted_iota(I32, (rows, page), 0)
        c2 = lax.broadcasted_iota(I32, (rows, page), 1)
        z = lax.dot_general(qbd, kb, nt, preferred_element_type=F32) + bias
        o, crun = _sb_columns(z, jnp.zeros((rows, LANES), F32), tri[:page, :page], c2 < (r2 % t_new),
                              lambda a: jnp.dot(a, vb, preferred_element_type=F32))
        o_acc[...] = o
        c_acc[...] = crun

    order = list(reversed(range(n_pg)))
    z = jnp.concatenate([jnp.dot(qbd, k_refs[pg][0].astype(BF16), preferred_element_type=F32) + bias
                         for pg in order], axis=1)

    def values(a):
        o = None
        for ci, pg in enumerate(order):
            od = lax.dot_general(a[:, ci * page:(ci + 1) * page], v_refs[pg][0].astype(BF16), nt,
                                 preferred_element_type=F32)
            o = od if o is None else o + od
        return o

    od, crun = _sb_columns(z, c_acc[...], tri, None, values)
    o = o_acc[...] + od
    o_acc[...] = o
    c_acc[...] = crun

    @pl.when(jj == pl.num_programs(1) - 1)
    def _():
        om = jnp.where(head_mask, o, 0.0)
        acc = om[0:t_new]
        for hh in range(1, n_heads):
            acc = acc + om[hh * t_new:(hh + 1) * t_new]
        o_ref[0] = acc


def _s_attn(q3, kn3, vn3, bias_rows, cache_kt, cache_vt, page_table):
    bd, t_new, d_sb = q3.shape
    n_pages = page_table.shape[1]
    page = cache_kt.shape[2]
    n_heads = d_sb // D_HEAD
    n_pg = min(PAGES_PER_STEP, n_pages)
    n_j = n_pages // n_pg
    rows = n_heads * t_new
    gw = min(2 * LANES, n_pg * page)
    jj = lax.broadcasted_iota(I32, (gw, gw), 0)
    ss = lax.broadcasted_iota(I32, (gw, gw), 1)
    tri = (jj > ss).astype(BF16)

    def page_spec(pg):
        return pl.BlockSpec((1, d_sb, page), lambda b, j, pt: (pt[b, n_pages - 1 - (j * n_pg + pg)], 0, 0))

    tok = pl.BlockSpec((1, t_new, d_sb), lambda b, j, pt: (b, 0, 0))
    grid_spec = pltpu.PrefetchScalarGridSpec(
        num_scalar_prefetch=1,
        grid=(bd, n_j),
        in_specs=[tok, tok, tok,
                  pl.BlockSpec((rows, LANES), lambda b, j, pt: (0, 0)),
                  pl.BlockSpec((gw, gw), lambda b, j, pt: (0, 0))]
                 + [page_spec(pg) for pg in range(n_pg)] * 2,
        out_specs=tok,
        scratch_shapes=[pltpu.VMEM((rows, d_sb), F32), pltpu.VMEM((rows, LANES), F32)])
    return pl.pallas_call(
        functools.partial(_s_attn_kernel, n_pg=n_pg, t_new=t_new, n_heads=n_heads),
        grid_spec=grid_spec,
        out_shape=jax.ShapeDtypeStruct((bd, t_new, d_sb), F32),
        compiler_params=_cparams(2),
        name="s_attn",
    )(page_table, q3, kn3, vn3, bias_rows, tri, *([cache_kt] * n_pg), *([cache_vt] * n_pg))


def _s_post_kernel(x_ref, sb_ref, u_ref, gv_ref, coef_ref, bt_ref, gt1_ref, sh2_ref, sc2_ref, gsb_ref, gml_ref,
                   wout_ref, g2_ref, wrt_ref, brt_ref, xm_ref, h2_ref, lg_ref, *, t_new):
    gv = gv_ref[...]
    mixed = bt_ref[...] + coef_ref[0] * gv
    for dd in range(1, t_new):
        mixed = mixed + coef_ref[dd] * pltpu.roll(gv, dd, 0)
    mlp = u_ref[...] * mixed
    _mix_tail(x_ref[...], sb_ref[...], mlp, gt1_ref[...], sh2_ref[...], sc2_ref[...], gsb_ref[...], gml_ref[...],
              wout_ref[...], g2_ref[...], wrt_ref[...], brt_ref[...], xm_ref, h2_ref, lg_ref)


def _s_post(x2, sb2, u2, gv2, coef, bias_t, gt1, sh2, sc2, gsb, gml, wout_bf, g2, wrt_bf, brt_b, t_new):
    n, d = x2.shape
    e = wrt_bf.shape[0]
    args = (x2, sb2, u2, gv2, coef, bias_t, gt1, sh2, sc2, gsb, gml, wout_bf, g2, wrt_bf, brt_b)
    full = lambda shape: pl.BlockSpec(shape, lambda i: (0,) * len(shape))
    return pl.pallas_call(
        functools.partial(_s_post_kernel, t_new=t_new),
        grid=(1,),
        in_specs=[full(a.shape) for a in args],
        out_specs=[full((n, d)), full((n * (d // LANES), LANES)), full((1, e, n))],
        out_shape=[jax.ShapeDtypeStruct((n, d), F32), jax.ShapeDtypeStruct((n * (d // LANES), LANES), F32),
                   jax.ShapeDtypeStruct((1, e, n), F32)],
        compiler_params=_cparams(1),
        name="s_post",
    )(*args)


def _route_kernel(lg_ref, tri_ref, p_ref, l_ref, n_ref, loc_ref, *, buf_rows, tile0):
    lg = lg_ref[0]
    n_e, tr = lg.shape
    eidx = lax.broadcasted_iota(I32, (n_e, tr), 0)
    work = lg
    cnt = jnp.zeros((n_e, tr), F32)
    sels, vals = [], []
    for kk in range(TOP_K):
        m = jnp.max(work, axis=0, keepdims=True)
        ek = jnp.min(jnp.where(work == m, eidx, n_e), axis=0, keepdims=True)
        sel = eidx == ek
        work = jnp.where(sel, -jnp.inf, work)
        cnt = cnt + sel.astype(F32)
        sels.append(sel)
        vals.append(m)
    ex = [jnp.exp(vv - vals[0]) for vv in vals]
    den = ex[0] + ex[1] + ex[2] + ex[3]
    for kk in range(TOP_K):
        p_ref[0, kk:kk + 1, :] = ex[kk] / den
    before = jnp.dot(cnt.astype(BF16), tri_ref[...], preferred_element_type=F32)
    n_tok = jnp.zeros((n_e, LANES), F32) + jnp.sum(cnt, axis=1, keepdims=True)
    room = jnp.floor((n_tok + (SEG_CHUNK - 1)) * (1.0 / SEG_CHUNK)) * SEG_CHUNK
    row = lax.broadcasted_iota(I32, (n_e, LANES), 0)
    incl = room
    shift = 1
    while shift < n_e:
        incl = incl + jnp.where(row >= shift, pltpu.roll(incl, shift, 0), 0.0)
        shift *= 2
    loc = incl - room
    tile = tile0 + pl.program_id(0) * pl.num_programs(1) + pl.program_id(1)
    slot_base = (lax.rem(tile, 2) * buf_rows).astype(F32)
    pos = (before + jnp.concatenate([loc] * (tr // LANES), axis=1) + slot_base) * ROW_TILE
    for kk in range(TOP_K):
        l_ref[0, kk:kk + 1, :] = jnp.sum(jnp.where(sels[kk], pos, 0.0), axis=0, keepdims=True).astype(I32)
    n_ref[0] = n_tok.astype(I32)
    loc_ref[0] = loc.astype(I32)


def _group_rows(tt, n_e):
    return TOP_K * tt + n_e * SEG_CHUNK


def _route(lgt, tt, tile0):
    b, n_e, s = lgt.shape
    n_i = s // tt
    t1 = lax.broadcasted_iota(I32, (tt, tt), 0)
    t2 = lax.broadcasted_iota(I32, (tt, tt), 1)
    tri = (t1 < t2).astype(BF16)
    tok = pl.BlockSpec((1, TOP_K, tt), lambda bb, ii: (bb * n_i + ii, 0, 0))
    tab = pl.BlockSpec((1, n_e, LANES), lambda bb, ii: (bb * n_i + ii, 0, 0))
    return pl.pallas_call(
        functools.partial(_route_kernel, buf_rows=_group_rows(tt, n_e), tile0=tile0),
        grid=(b, n_i),
        in_specs=[pl.BlockSpec((1, n_e, tt), lambda bb, ii: (bb, 0, ii)),
                  pl.BlockSpec((tt, tt), lambda bb, ii: (0, 0))],
        out_specs=[tok, tok, tab, tab],
        out_shape=[jax.ShapeDtypeStruct((b * n_i, TOP_K, tt), F32), jax.ShapeDtypeStruct((b * n_i, TOP_K, tt), I32),
                   jax.ShapeDtypeStruct((b * n_i, n_e, LANES), I32), jax.ShapeDtypeStruct((b * n_i, n_e, LANES), I32)],
        compiler_params=_cparams(2),
        name="route",
    )(lgt, tri)


def _rows(ref, row, n_rows):
    return ref.at[pl.ds(pl.multiple_of(row * ROW_TILE, ROW_TILE), n_rows * ROW_TILE)]


def _segment_copies(n_ref, loc_ref, gs_ref, tile, n_e, fn):
    def per_expert(ee, carry):
        idx = tile * n_e + ee
        n_chunks = lax.shift_right_logical(n_ref[idx] + (SEG_CHUNK - 1), SEG_CHUNK.bit_length() - 1)

        def per_chunk(j, c):
            fn(loc_ref[idx] + j * SEG_CHUNK, gs_ref[idx] + j * SEG_CHUNK)
            return c

        lax.fori_loop(0, n_chunks, per_chunk, 0)
        return carry

    lax.fori_loop(0, n_e, per_expert, 0)


def _dispatch_kernel(n_ref, loc_ref, gs_ref, sz_ref, pd_ref, ps_ref, nu_ref, l_ref, hp_ref, hs_ref, xs_ref, srt,
                     zero_ref, sem, zsem, *, tt, n_e, bm, n_blocks, n_p):
    tile = pl.program_id(0)
    slot = lax.rem(tile, 2)
    buf_rows = _group_rows(tt, n_e)

    def fill_zero(first_row, n_rows):
        n_chunks = lax.shift_right_logical(n_rows + (SEG_CHUNK - 1), SEG_CHUNK.bit_length() - 1)

        def chunk(j):
            return pltpu.make_async_copy(zero_ref, _rows(xs_ref, first_row + j * SEG_CHUNK, SEG_CHUNK), zsem)

        lax.fori_loop(0, n_chunks, lambda j, c: (chunk(j).start(), c)[1], 0)
        lax.fori_loop(0, n_chunks, lambda j, c: (chunk(j).wait(), c)[1], 0)

    @pl.when(tile == 0)
    def _():
        zero_ref[...] = jnp.zeros_like(zero_ref)
        srt[...] = jnp.zeros_like(srt)

        def per_expert(ee, carry):
            fill_zero(ps_ref[ee] + sz_ref[ee], pd_ref[ee] - sz_ref[ee])
            return carry

        lax.fori_loop(0, n_e, per_expert, 0)
        fill_zero(nu_ref[0] * bm, (n_blocks - nu_ref[0]) * bm)

    def scatter_from(h_ref):
        def scatter(t8, carry):
            base = t8 * SUBLANES
            rows = [h_ref[pl.ds(pl.multiple_of((base + u) * ROW_TILE, ROW_TILE), ROW_TILE), :]
                    for u in range(SUBLANES)]
            for u in range(SUBLANES):
                for kk in range(TOP_K):
                    srt[pl.ds(pl.multiple_of(l_ref[0, kk, base + u], ROW_TILE), ROW_TILE), :] = rows[u]
            return carry

        lax.fori_loop(0, tt // SUBLANES, scatter, 0)

    pl.when(tile < n_p)(lambda: scatter_from(hp_ref))
    pl.when(tile >= n_p)(lambda: scatter_from(hs_ref))

    def copy(s):
        return lambda lrow, grow: pltpu.make_async_copy(_rows(srt, s * buf_rows + lrow, SEG_CHUNK),
                                                        _rows(xs_ref, grow, SEG_CHUNK), sem.at[s])

    @pl.when(tile > 0)
    def _():
        mk = copy(1 - slot)
        _segment_copies(n_ref, loc_ref, gs_ref, tile - 1, n_e, lambda lrow, grow: mk(lrow, grow).wait())

    mk = copy(slot)
    _segment_copies(n_ref, loc_ref, gs_ref, tile, n_e, lambda lrow, grow: mk(lrow, grow).start())

    @pl.when(tile == pl.num_programs(0) - 1)
    def _():
        _segment_copies(n_ref, loc_ref, gs_ref, tile, n_e, lambda lrow, grow: mk(lrow, grow).wait())


def _dispatch(tables, l_t, h2_p, h2_s, n_blocks, bm):
    n_tiles, kk, tt = l_t.shape
    n_p = n_tiles - 1
    n_e = tables[3].shape[0]
    grid_spec = pltpu.PrefetchScalarGridSpec(
        num_scalar_prefetch=7,
        grid=(n_tiles,),
        in_specs=[pl.BlockSpec((1, kk, tt), lambda t, *_: (t, 0, 0), memory_space=pltpu.SMEM),
                  pl.BlockSpec((tt * ROW_TILE, LANES), lambda t, *_: (jnp.minimum(t, n_p - 1), 0)),
                  pl.BlockSpec((tt * ROW_TILE, LANES), lambda t, *_: (0, 0))],
        out_specs=pl.BlockSpec(memory_space=pl.ANY),
        scratch_shapes=[pltpu.VMEM((2 * _group_rows(tt, n_e) * ROW_TILE, LANES), F32),
                        pltpu.VMEM((SEG_CHUNK * ROW_TILE, LANES), F32),
                        pltpu.SemaphoreType.DMA((2,)), pltpu.SemaphoreType.DMA(())])
    return pl.pallas_call(
        functools.partial(_dispatch_kernel, tt=tt, n_e=n_e, bm=bm, n_blocks=n_blocks, n_p=n_p),
        grid_spec=grid_spec,
        out_shape=jax.ShapeDtypeStruct((n_blocks * bm * ROW_TILE, LANES), F32),
        compiler_params=_cparams(1),
        name="dispatch",
    )(*tables, l_t, h2_p, h2_s)


def _expert_kernel(be_ref, nu_ref, xs_ref, wgu_ref, bgu_ref, wd_ref, bd_ref, yb_ref, *, bm, d, d_ff):
    blk = pl.program_id(0)
    n_c = d // LANES

    @pl.when(blk < nu_ref[0])
    def _():
        x = jnp.concatenate([xs_ref[pl.ds(c, bm, stride=n_c), :] for c in range(n_c)], axis=1)
        gu = jnp.dot(x.astype(BF16), wgu_ref[0].astype(BF16), preferred_element_type=F32) + bgu_ref[0]
        gate = jnp.minimum(gu[:, :d_ff], SWIGLU_LIMIT)
        up = jnp.clip(gu[:, d_ff:], -SWIGLU_LIMIT, SWIGLU_LIMIT)
        act = gate * (1.0 / (1.0 + jnp.exp(-SWIGLU_ALPHA * gate))) * (up + 1.0)
        y = jnp.dot(act.astype(BF16), wd_ref[0].astype(BF16), preferred_element_type=F32) + bd_ref[0]
        for c in range(n_c):
            yb_ref[pl.ds(c, bm, stride=n_c), :] = y[:, c * LANES:(c + 1) * LANES]

    @pl.when(blk >= nu_ref[0])
    def _():
        yb_ref[...] = jnp.zeros_like(yb_ref)


def _experts(block_e, n_used, xs_rows, wgu, bgu, wd, bd, n_blocks, bm):
    n_e, d, d_ff2 = wgu.shape
    d_ff = d_ff2 // 2
    grid_spec = pltpu.PrefetchScalarGridSpec(
        num_scalar_prefetch=2,
        grid=(n_blocks,),
        in_specs=[pl.BlockSpec((bm * ROW_TILE, LANES), lambda i, be, nu: (i, 0)),
                  pl.BlockSpec((1, d, d_ff2), lambda i, be, nu: (be[i], 0, 0)),
                  pl.BlockSpec((1, 1, d_ff2), lambda i, be, nu: (be[i], 0, 0)),
                  pl.BlockSpec((1, d_ff, d), lambda i, be, nu: (be[i], 0, 0)),
                  pl.BlockSpec((1, 1, d), lambda i, be, nu: (be[i], 0, 0))],
        out_specs=pl.BlockSpec((bm * ROW_TILE, LANES), lambda i, be, nu: (i, 0)))
    return pl.pallas_call(
        functools.partial(_expert_kernel, bm=bm, d=d, d_ff=d_ff),
        grid_spec=grid_spec,
        out_shape=jax.ShapeDtypeStruct((n_blocks * bm * ROW_TILE, LANES), F32),
        compiler_params=_cparams(1),
        name="experts",
    )(block_e, n_used, xs_rows, wgu, bgu.reshape(n_e, 1, d_ff2), wd, bd.reshape(n_e, 1, d))


def _combine_kernel(n_ref, loc_ref, gs_ref, l_ref, p_ref, xmp_ref, gtp_ref, xms_ref, gts_ref, gf_ref, yb_ref,
                    yp_ref, ys_ref, gat, ytile, sem, *, tt, n_e, d, n_p):
    tile = pl.program_id(0)
    slot = lax.rem(tile, 2)
    buf_rows = _group_rows(tt, n_e)
    n_c = d // LANES

    def copy(s):
        return lambda lrow, grow: pltpu.make_async_copy(_rows(yb_ref, grow, SEG_CHUNK),
                                                        _rows(gat, s * buf_rows + lrow, SEG_CHUNK), sem.at[s])

    @pl.when(tile == 0)
    def _():
        mk = copy(slot)
        _segment_copies(n_ref, loc_ref, gs_ref, tile, n_e, lambda lrow, grow: mk(lrow, grow).start())

    @pl.when(tile + 1 < pl.num_programs(0))
    def _():
        mk = copy(1 - slot)
        _segment_copies(n_ref, loc_ref, gs_ref, tile + 1, n_e, lambda lrow, grow: mk(lrow, grow).start())

    mk = copy(slot)
    _segment_copies(n_ref, loc_ref, gs_ref, tile, n_e, lambda lrow, grow: mk(lrow, grow).wait())

    def gather(t8, carry):
        base = t8 * SUBLANES
        for u in range(SUBLANES):
            t = base + u
            acc = None
            for kk in range(TOP_K):
                term = p_ref[0, kk, t] * gat[pl.ds(pl.multiple_of(l_ref[0, kk, t], ROW_TILE), ROW_TILE), :]
                acc = term if acc is None else acc + term
            ytile[pl.ds(pl.multiple_of(t * ROW_TILE, ROW_TILE), ROW_TILE), :] = acc
        return carry

    lax.fori_loop(0, tt // SUBLANES, gather, 0)
    def finish(xm_ref, gt_ref, y_ref):
        y = jnp.concatenate([ytile[pl.ds(c, tt, stride=n_c), :] for c in range(n_c)], axis=1)
        xo = xm_ref[0] + gt_ref[0] * y
        y_ref[0] = _rms(xo) * gf_ref[...]

    pl.when(tile < n_p)(lambda: finish(xmp_ref, gtp_ref, yp_ref))
    pl.when(tile >= n_p)(lambda: finish(xms_ref, gts_ref, ys_ref))


def _combine(tables, l_t, p_t, xm_p, gt_p, xm_s, gt_s, gfin, yb_rows):
    n_tiles, kk, tt = l_t.shape
    n_p = n_tiles - 1
    b, s, d = xm_p.shape
    n_i = s // tt
    n_e = tables[0].shape[0] // n_tiles
    tp = lambda t: jnp.minimum(t, n_p - 1)
    tok_smem = pl.BlockSpec((1, kk, tt), lambda t, *_: (t, 0, 0), memory_space=pltpu.SMEM)
    p_blk = pl.BlockSpec((1, tt, d), lambda t, *_: (tp(t) // n_i, tp(t) % n_i, 0))
    s_blk = pl.BlockSpec((1, tt, d), lambda t, *_: (0, 0, 0))
    grid_spec = pltpu.PrefetchScalarGridSpec(
        num_scalar_prefetch=3,
        grid=(n_tiles,),
        in_specs=[tok_smem, tok_smem, p_blk,
                  pl.BlockSpec((1, 1, d), lambda t, *_: (tp(t) // n_i, 0, 0)),
                  s_blk, s_blk,
                  pl.BlockSpec((1, d), lambda t, *_: (0, 0)),
                  pl.BlockSpec(memory_space=pl.ANY)],
        out_specs=[p_blk, s_blk],
        scratch_shapes=[pltpu.VMEM((2 * _group_rows(tt, n_e) * ROW_TILE, LANES), F32),
                        pltpu.VMEM((tt * ROW_TILE, LANES), F32),
                        pltpu.SemaphoreType.DMA((2,))])
    return pl.pallas_call(
        functools.partial(_combine_kernel, tt=tt, n_e=n_e, d=d, n_p=n_p),
        grid_spec=grid_spec,
        out_shape=[jax.ShapeDtypeStruct((b, s, d), F32), jax.ShapeDtypeStruct(xm_s.shape, F32)],
        compiler_params=_cparams(1),
        name="combine",
    )(*tables[:3], l_t, p_t, xm_p, gt_p, xm_s, gt_s, gfin, yb_rows)


def _moe(lg_p, h2_p, xm_p, gt_p, lg_s, h2_s, xm_s, gt_s, gfin, wgu, bgu, wd, bd):
    b, n_e, s = lg_p.shape
    tt = min(TT_MOE, s)
    assert lg_s.shape == (1, n_e, tt), "the sample group must be exactly one routing tile"
    n_p = b * (s // tt)
    n_assign = (n_p + 1) * tt * TOP_K
    bm = BM_EXPERT if n_assign >= n_e * BM_EXPERT else BM_EXPERT_SMALL
    n_blocks = -(-(n_assign + n_e * (SEG_CHUNK - 1)) // bm) + n_e + 1
    routed = [_route(lg_p, tt, 0), _route(lg_s, tt, n_p)]
    p_t, l_t, n_tile, loc_tile = [jnp.concatenate([rp, rs], axis=0) for rp, rs in zip(*routed)]
    n2 = n_tile[:, :, 0]
    sizes = jnp.sum(n2, axis=0)
    padded = ((sizes + (SEG_CHUNK - 1) + bm - 1) // bm) * bm
    pad_end = jnp.cumsum(padded)
    pad_start = pad_end - padded
    g_start = pad_start[None, :] + jnp.cumsum(n2, axis=0) - n2
    block_start = jnp.arange(n_blocks, dtype=I32) * bm
    block_e = jnp.minimum(jnp.sum((pad_end[None, :] <= block_start[:, None]).astype(I32), axis=1), n_e - 1)
    n_used = (pad_end[-1:] // bm).astype(I32)
    tables = (n2.reshape(-1), loc_tile[:, :, 0].reshape(-1), g_start.reshape(-1).astype(I32), sizes, padded,
              pad_start, n_used)
    xs_rows = _dispatch(tables, l_t, h2_p, h2_s, n_blocks, bm)
    yb_rows = _experts(block_e, n_used, xs_rows, wgu, bgu, wd, bd, n_blocks, bm)
    return _combine(tables, l_t, p_t, xm_p, gt_p, xm_s, gt_s, gfin, yb_rows)


def kernel(x_prompt, x_sample, cache_k, cache_v, page_table, c_prompt, c_sample, w_ada, b_ada, g_norm_mix,
           g_norm_ffn, w_in, b_sb, w_s, b_s, g_sb_out, g_mlp_out, w_out, w_router, b_router, w_gate_up,
           b_gate_up, w_down, b_down, g_final):
    b, s, d = x_prompt.shape
    bd, t_new, _ = x_sample.shape
    depth, n_pool, page, h_sb, d_head = cache_k.shape
    assert depth == 1 and d_head == D_HEAD
    d_sb = h_sb * d_head
    d_mlp = g_mlp_out.shape[1]
    n_e = w_router.shape[2]
    n_tok_s = bd * t_new

    win_bf = w_in[0].astype(BF16)
    wout_bf = w_out[0].astype(BF16)
    wada_bf = w_ada[0].astype(BF16)
    wrt_bf = jnp.transpose(w_router[0]).astype(BF16)
    brt_b = jnp.broadcast_to(b_router[0][:, None], (n_e, LANES))
    g1 = g_norm_mix[0].reshape(1, d)
    g2 = g_norm_ffn[0].reshape(1, d)
    gsb = g_sb_out[0].reshape(1, d_sb)
    gml = g_mlp_out[0].reshape(1, d_mlp)
    gfin = g_final.reshape(1, d)
    bs_full = jnp.repeat(jnp.transpose(b_s[0]), d_head, axis=1)
    bias_rows = jnp.broadcast_to(jnp.repeat(b_sb[0], t_new)[:, None], (h_sb * t_new, LANES))

    ada = _ada(jnp.concatenate([c_prompt, c_sample], axis=0), wada_bf, b_ada[0])
    ada_p = ada[:b].reshape(b, N_ADA, d)
    ada_s = jnp.repeat(ada[b:].reshape(bd, N_ADA, d), t_new, axis=0)

    k_p, v_p, xm_p, h2_p, lg_p = _mix_prompt(x_prompt, ada_p, b_sb[0], g1, win_bf, w_s[0], bs_full, gsb, gml,
                                             wout_bf, g2, wrt_bf, brt_b, page)

    x_s2 = x_sample.reshape(n_tok_s, d)
    q_s, k_s, v_s, u_s, gv_s = _s_pre(x_s2, ada_s[:, 0], ada_s[:, 1], g1, win_bf, d_sb, d_mlp)
    cache_kt = jnp.transpose(cache_k[0], (0, 2, 3, 1)).reshape(n_pool, d_sb, page)
    cache_vt = jnp.transpose(cache_v[0], (0, 2, 3, 1)).reshape(n_pool, d_sb, page)
    sb_s = _s_attn(q_s.reshape(bd, t_new, d_sb), k_s.reshape(bd, t_new, d_sb), v_s.reshape(bd, t_new, d_sb),
                   bias_rows, cache_kt, cache_vt, page_table)
    w_tril = jnp.tril(w_s[0][:, :t_new, :t_new])
    tt = jnp.arange(t_new)
    coef = jnp.stack([jnp.where((tt >= dd)[None, :], w_tril[:, tt, jnp.maximum(tt - dd, 0)], 0.0)
                      for dd in range(t_new)])
    coef = jnp.tile(jnp.repeat(jnp.transpose(coef, (0, 2, 1)), d_head, axis=2), (1, bd, 1))
    bias_t = jnp.tile(jnp.repeat(jnp.transpose(b_s[0][:, :t_new]), d_head, axis=1), (bd, 1))
    xm_s, h2_s, lg_s = _s_post(x_s2, sb_s.reshape(n_tok_s, d_sb), u_s, gv_s, coef, bias_t, ada_s[:, 2],
                               ada_s[:, 3], ada_s[:, 4], gsb, gml, wout_bf, g2, wrt_bf, brt_b, t_new)

    y_prompt, y_sample = _moe(lg_p, h2_p, xm_p, ada_p[:, 5:6, :], lg_s, h2_s, xm_s.reshape(1, n_tok_s, d),
                              ada_s[:, 5].reshape(1, n_tok_s, d), gfin, w_gate_up[0], b_gate_up[0], w_down[0],
                              b_down[0])

    return (y_prompt, y_sample.reshape(bd, t_new, d),
            jnp.transpose(k_p.reshape(b, s // page, h_sb, d_head, page), (0, 1, 4, 2, 3))[None],
            jnp.transpose(v_p.reshape(b, s // page, h_sb, d_head, page), (0, 1, 4, 2, 3))[None],
            k_s.reshape(1, bd, t_new, h_sb, d_head), v_s.reshape(1, bd, t_new, h_sb, d_head),
            gv_s.reshape(1, bd, t_new, d_mlp))
```

```python
import functools
import math

import jax
import jax.numpy as jnp
from jax import lax
from jax.experimental import pallas as pl
from jax.experimental.pallas import tpu as pltpu

F32 = jnp.float32
BF16 = jnp.bfloat16
I32 = jnp.int32

D_HEAD = 64
TOP_K = 4
N_ADA = 6
EPS = 1e-6
SWIGLU_LIMIT = 7.0
SWIGLU_ALPHA = 1.702
LANES = 128
SUBLANES = 8
ROW_TILE = SUBLANES
VMEM_LIMIT = 56 * 1024 * 1024

TS_MIX = 256
TT_MOE = 512
SEG_CHUNK = 32
BM_EXPERT = 512
BM_EXPERT_SMALL = 128
PAGES_PER_STEP = 32


def _cparams(n_axes):
    return pltpu.CompilerParams(dimension_semantics=("arbitrary",) * n_axes, vmem_limit_bytes=VMEM_LIMIT)


def _rms(x):
    return x * lax.rsqrt(jnp.mean(x * x, axis=-1, keepdims=True) + EPS)


def _gelu(x):
    c = math.sqrt(2.0 / math.pi)
    return x * (0.5 * (1.0 + jnp.tanh(c * (x + 0.044715 * (x * x * x)))))


def _log_sigmoid_pair(z):
    t = jnp.log(1.0 + jnp.exp(-jnp.abs(z)))
    lsig = jnp.minimum(z, 0.0) - t
    return lsig, lsig - z


def _sb_weights(z, crun, triu, mask):
    lsig, ls = _log_sigmoid_pair(z)
    if mask is not None:
        ls = jnp.where(mask, ls, 0.0)
    after = jnp.dot(ls.astype(BF16), triu, preferred_element_type=F32)
    cr = jnp.concatenate([crun] * (z.shape[1] // LANES), axis=1) if z.shape[1] > LANES else crun
    a = jnp.exp(lsig + after + cr)
    if mask is not None:
        a = jnp.where(mask, a, 0.0)
    return a, crun + jnp.sum(ls, axis=1, keepdims=True)


def _ada_kernel(c_ref, w_ref, b_ref, o_ref):
    c = c_ref[...]
    s = c * (1.0 / (1.0 + jnp.exp(-c)))
    o_ref[...] = jnp.dot(s.astype(BF16), w_ref[...], preferred_element_type=F32) + b_ref[...]


def _ada(c_all, w_ada_bf, b_ada):
    r, d = c_all.shape
    n = w_ada_bf.shape[1]
    return pl.pallas_call(
        _ada_kernel,
        grid=(n // d,),
        in_specs=[pl.BlockSpec((r, d), lambda j: (0, 0)),
                  pl.BlockSpec((d, d), lambda j: (0, j)),
                  pl.BlockSpec((1, d), lambda j: (0, j))],
        out_specs=pl.BlockSpec((r, d), lambda j: (0, j)),
        out_shape=jax.ShapeDtypeStruct((r, n), F32),
        compiler_params=_cparams(1),
        name="ada",
    )(c_all, w_ada_bf, b_ada.reshape(1, n))


def _mix_tail(x, sb, mlp, gt1, sh2, sc2, gsb, gml, wout, g2, wrt, brt, xm_ref, h2_ref, lg_ref):
    rows, d = x.shape
    o = jnp.concatenate([_rms(sb) * gsb, _rms(mlp) * gml], axis=1).astype(BF16)
    xm = x + gt1 * jnp.dot(o, wout, preferred_element_type=F32)
    xm_ref[...] = xm.reshape(xm_ref.shape)
    h2 = _rms(xm) * g2 * (1.0 + sc2) + sh2
    for c in range(d // LANES):
        h2_ref[pl.ds(c, rows, stride=d // LANES), :] = h2[:, c * LANES:(c + 1) * LANES]
    lg = lax.dot_general(wrt, h2.astype(BF16), (((1,), (1,)), ((), ())), preferred_element_type=F32)
    lg = lg + jnp.concatenate([brt] * (rows // LANES), axis=1)
    lg_ref[...] = lg.reshape(lg_ref.shape)


def _mix_prompt_kernel(bsb_ref, x_ref, ada_ref, g1_ref, win_ref, ws_ref, bsf_ref, triu_ref, gsb_ref, gml_ref,
                       wout_ref, g2_ref, wrt_ref, brt_ref,
                       k_ref, v_ref, xm_ref, h2_ref, lg_ref,
                       kt_buf, v_buf, o_acc, c_acc, *, ts, n_pairs, d_sb, d_mlp):
    i = pl.program_id(1)
    x = x_ref[0]
    ada = ada_ref[0]
    sh1, sc1, gt1, sh2, sc2 = ada[0:1], ada[1:2], ada[2:3], ada[3:4], ada[4:5]
    h = _rms(x) * g1_ref[...] * (1.0 + sc1) + sh1
    p = jnp.dot(h.astype(BF16), win_ref[...], preferred_element_type=F32)
    q = p[:, :d_sb] * (D_HEAD ** -0.5)
    k = p[:, d_sb:2 * d_sb]
    v = p[:, 2 * d_sb:3 * d_sb]
    u = _gelu(p[:, 3 * d_sb:3 * d_sb + d_mlp])
    gv = _gelu(p[:, 3 * d_sb + d_mlp:])
    kt = jnp.transpose(k)
    vt = jnp.transpose(v)
    page = k_ref.shape[3]
    for r in range(ts // page):
        k_ref[0, r] = kt[:, r * page:(r + 1) * page]
        v_ref[0, r] = vt[:, r * page:(r + 1) * page]
    kt_buf[i] = kt.astype(BF16)
    v_buf[i] = v.astype(BF16)

    lane = lax.broadcasted_iota(I32, (ts, LANES), 1)
    lo = lane < D_HEAD
    row = lax.broadcasted_iota(I32, (2 * ts, ts), 0)
    col = lax.broadcasted_iota(I32, (2 * ts, ts), 1)
    diag_mask = col < jnp.where(row >= ts, row - ts, row)
    triu = triu_ref[...]
    qs = []
    for j in range(n_pairs):
        q2 = q[:, j * LANES:(j + 1) * LANES]
        qs.append(jnp.concatenate([jnp.where(lo, q2, 0.0), jnp.where(lo, 0.0, q2)], axis=0).astype(BF16))

    def pair_unit(j, c, crun, mask):
        kt = kt_buf[c, j * LANES:(j + 1) * LANES, :]
        vv = v_buf[c, :, j * LANES:(j + 1) * LANES]
        z = jnp.dot(qs[j], kt, preferred_element_type=F32)
        z = jnp.concatenate([z[:ts] + bsb_ref[2 * j], z[ts:] + bsb_ref[2 * j + 1]], axis=0)
        a, crun = _sb_weights(z, crun, triu, mask)
        return jnp.dot(a.astype(BF16), vv, preferred_element_type=F32), crun

    for j in range(n_pairs):
        o, crun = pair_unit(j, i, jnp.zeros((2 * ts, LANES), F32), diag_mask)
        o_acc[j] = o
        c_acc[j] = crun

    def chunk_body(it, carry):
        c = i - 1 - it
        for j in range(n_pairs):
            o, crun = pair_unit(j, c, c_acc[j], None)
            o_acc[j] = o_acc[j] + o
            c_acc[j] = crun
        return carry

    lax.fori_loop(0, i, chunk_body, 0)
    sb = jnp.concatenate([jnp.where(lo, o_acc[j, :ts], o_acc[j, ts:]) for j in range(n_pairs)], axis=1)

    chunk = ws_ref.shape[1]
    tr = lax.broadcasted_iota(I32, (chunk, chunk), 0)
    tc = lax.broadcasted_iota(I32, (chunk, chunk), 1)
    tril = tc <= tr
    lo_c = lax.broadcasted_iota(I32, (chunk, LANES), 1) < D_HEAD
    ws = [jnp.where(tril, ws_ref[g], 0.0).astype(BF16) for g in range(2 * n_pairs)]
    mlp_rows = []
    for r in range(ts // chunk):
        cols = []
        for j in range(n_pairs):
            gv2 = gv[r * chunk:(r + 1) * chunk, j * LANES:(j + 1) * LANES].astype(BF16)
            ma = jnp.dot(ws[2 * j], gv2, preferred_element_type=F32)
            mb = jnp.dot(ws[2 * j + 1], gv2, preferred_element_type=F32)
            mixed = jnp.where(lo_c, ma, mb) + bsf_ref[:, j * LANES:(j + 1) * LANES]
            cols.append(u[r * chunk:(r + 1) * chunk, j * LANES:(j + 1) * LANES] * mixed)
        mlp_rows.append(jnp.concatenate(cols, axis=1))
    mlp = jnp.concatenate(mlp_rows, axis=0)

    _mix_tail(x, sb, mlp, gt1, sh2, sc2, gsb_ref[...], gml_ref[...], wout_ref[...], g2_ref[...],
              wrt_ref[...], brt_ref[...], xm_ref, h2_ref, lg_ref)


def _mix_prompt(x, ada_p, b_sb, g1, win_bf, w_s, bs_full, gsb, gml, wout_bf, g2, wrt_bf, brt_b, page):
    b, s, d = x.shape
    ts = min(TS_MIX, s)
    n_i = s // ts
    d_sb = gsb.shape[1]
    d_mlp = gml.shape[1]
    n_pairs = d_sb // LANES
    e = wrt_bf.shape[0]
    jj = lax.broadcasted_iota(I32, (ts, ts), 0)
    ss = lax.broadcasted_iota(I32, (ts, ts), 1)
    triu = (jj > ss).astype(BF16)
    const = lambda shape: pl.BlockSpec(shape, lambda bb, ii, *_: (0,) * len(shape))
    grid_spec = pltpu.PrefetchScalarGridSpec(
        num_scalar_prefetch=1,
        grid=(b, n_i),
        in_specs=[pl.BlockSpec((1, ts, d), lambda bb, ii, *_: (bb, ii, 0)),
                  pl.BlockSpec((1, N_ADA, d), lambda bb, ii, *_: (bb, 0, 0)),
                  const((1, d)), const(win_bf.shape), const(w_s.shape), const(bs_full.shape), const((ts, ts)),
                  const((1, d_sb)), const((1, d_mlp)), const(wout_bf.shape), const((1, d)),
                  const(wrt_bf.shape), const(brt_b.shape)],
        out_specs=[pl.BlockSpec((1, ts // page, d_sb, page), lambda bb, ii, *_: (bb, ii, 0, 0)),
                   pl.BlockSpec((1, ts // page, d_sb, page), lambda bb, ii, *_: (bb, ii, 0, 0)),
                   pl.BlockSpec((1, ts, d), lambda bb, ii, *_: (bb, ii, 0)),
                   pl.BlockSpec((ts * (d // LANES), LANES), lambda bb, ii, *_: (bb * n_i + ii, 0)),
                   pl.BlockSpec((1, e, ts), lambda bb, ii, *_: (bb, 0, ii))],
        scratch_shapes=[pltpu.VMEM((n_i, d_sb, ts), BF16), pltpu.VMEM((n_i, ts, d_sb), BF16),
                        pltpu.VMEM((n_pairs, 2 * ts, LANES), F32), pltpu.VMEM((n_pairs, 2 * ts, LANES), F32)])
    return pl.pallas_call(
        functools.partial(_mix_prompt_kernel, ts=ts, n_pairs=n_pairs, d_sb=d_sb, d_mlp=d_mlp),
        grid_spec=grid_spec,
        out_shape=[jax.ShapeDtypeStruct((b, s // page, d_sb, page), F32),
                   jax.ShapeDtypeStruct((b, s // page, d_sb, page), F32),
                   jax.ShapeDtypeStruct((b, s, d), F32),
                   jax.ShapeDtypeStruct((b * s * (d // LANES), LANES), F32),
                   jax.ShapeDtypeStruct((b, e, s), F32)],
        compiler_params=_cparams(2),
        name="mix_prompt",
    )(b_sb, x, ada_p, g1, win_bf, w_s, bs_full, triu, gsb, gml, wout_bf, g2, wrt_bf, brt_b)


def _s_pre_kernel(x_ref, sh_ref, sc_ref, g1_ref, win_ref, q_ref, k_ref, v_ref, u_ref, gv_ref, *, d_sb, d_mlp):
    h = _rms(x_ref[...]) * g1_ref[...] * (1.0 + sc_ref[...]) + sh_ref[...]
    p = jnp.dot(h.astype(BF16), win_ref[...], preferred_element_type=F32)
    q_ref[...] = p[:, :d_sb] * (D_HEAD ** -0.5)
    k_ref[...] = p[:, d_sb:2 * d_sb]
    v_ref[...] = p[:, 2 * d_sb:3 * d_sb]
    u_ref[...] = _gelu(p[:, 3 * d_sb:3 * d_sb + d_mlp])
    gv_ref[...] = _gelu(p[:, 3 * d_sb + d_mlp:])


def _s_pre(x2, sh1, sc1, g1, win_bf, d_sb, d_mlp):
    n, d = x2.shape
    full = lambda shape: pl.BlockSpec(shape, lambda i: (0,) * len(shape))
    return pl.pallas_call(
        functools.partial(_s_pre_kernel, d_sb=d_sb, d_mlp=d_mlp),
        grid=(1,),
        in_specs=[full((n, d)), full((n, d)), full((n, d)), full((1, d)), full(win_bf.shape)],
        out_specs=[full((n, d_sb))] * 3 + [full((n, d_mlp))] * 2,
        out_shape=[jax.ShapeDtypeStruct((n, d_sb), F32)] * 3 + [jax.ShapeDtypeStruct((n, d_mlp), F32)] * 2,
        compiler_params=_cparams(1),
        name="s_pre",
    )(x2, sh1, sc1, g1, win_bf)


def _sb_columns(z, crun, tri, mask, out_fn):
    gw = tri.shape[0]
    n_g = z.shape[1] // gw
    rows = z.shape[0]
    lsig, ls = _log_sigmoid_pair(z)
    if mask is not None:
        ls = jnp.where(mask, ls, 0.0)
    parts = [ls[:, g * gw:(g + 1) * gw] for g in range(n_g)]
    aft = jnp.dot(jnp.concatenate(parts, axis=0).astype(BF16), tri, preferred_element_type=F32)
    cols = [None] * n_g
    for g in reversed(range(n_g)):
        cols[g] = aft[g * rows:(g + 1) * rows] + jnp.concatenate([crun] * (gw // LANES), axis=1)
        crun = crun + jnp.sum(parts[g], axis=1, keepdims=True)
    a = jnp.exp(lsig + jnp.concatenate(cols, axis=1))
    if mask is not None:
        a = jnp.where(mask, a, 0.0)
    return out_fn(a.astype(BF16)), crun


def _s_attn_kernel(pt_ref, q_ref, kn_ref, vn_ref, bias_ref, tri_ref, *refs, n_pg, t_new, n_heads):
    k_refs = refs[:n_pg]
    v_refs = refs[n_pg:2 * n_pg]
    o_ref = refs[2 * n_pg]
    o_acc, c_acc = refs[2 * n_pg + 1:]
    jj = pl.program_id(1)
    rows = n_heads * t_new
    d_sb = q_ref.shape[2]
    page = k_refs[0].shape[2]
    q_rep = jnp.concatenate([q_ref[0]] * n_heads, axis=0)
    rr = lax.broadcasted_iota(I32, (rows, d_sb), 0)
    ll = lax.broadcasted_iota(I32, (rows, d_sb), 1)
    head_mask = (ll // D_HEAD) == (rr // t_new)
    qbd = jnp.where(head_mask, q_rep, 0.0).astype(BF16)
    bias = bias_ref[...]
    tri = tri_ref[...]
    nt = (((1,), (1,)), ((), ()))

    @pl.when(jj == 0)
    def _():
        pad = jnp.zeros((page - t_new, d_sb), F32)
        kb = jnp.concatenate([kn_ref[0], pad], axis=0).astype(BF16)
        vb = jnp.concatenate([vn_ref[0], pad], axis=0).astype(BF16)
        r2 = lax.broadcasted_iota(I32, (rows, page), 0)
        c2 = lax.broadcasted_iota(I32, (rows, page), 1)
        z = lax.dot_general(qbd, kb, nt, preferred_element_type=F32) + bias
        o, crun = _sb_columns(z, jnp.zeros((rows, LANES), F32), tri[:page, :page], c2 < (r2 % t_new),
                              lambda a: jnp.dot(a, vb, preferred_element_type=F32))
        o_acc[...] = o
        c_acc[...] = crun

    order = list(reversed(range(n_pg)))
    z = jnp.concatenate([jnp.dot(qbd, k_refs[pg][0].astype(BF16), preferred_element_type=F32) + bias
                         for pg in order], axis=1)

    def values(a):
        o = None
        for ci, pg in enumerate(order):
            od = lax.dot_general(a[:, ci * page:(ci + 1) * page], v_refs[pg][0].astype(BF16), nt,
                                 preferred_element_type=F32)
            o = od if o is None else o + od
        return o

    od, crun = _sb_columns(z, c_acc[...], tri, None, values)
    o = o_acc[...] + od
    o_acc[...] = o
    c_acc[...] = crun

    @pl.when(jj == pl.num_programs(1) - 1)
    def _():
        om = jnp.where(head_mask, o, 0.0)
        acc = om[0:t_new]
        for hh in range(1, n_heads):
            acc = acc + om[hh * t_new:(hh + 1) * t_new]
        o_ref[0] = acc


def _s_attn(q3, kn3, vn3, bias_rows, cache_kt, cache_vt, page_table):
    bd, t_new, d_sb = q3.shape
    n_pages = page_table.shape[1]
    page = cache_kt.shape[2]
    n_heads = d_sb // D_HEAD
    n_pg = min(PAGES_PER_STEP, n_pages)
    n_j = n_pages // n_pg
    rows = n_heads * t_new
    gw = min(2 * LANES, n_pg * page)
    jj = lax.broadcasted_iota(I32, (gw, gw), 0)
    ss = lax.broadcasted_iota(I32, (gw, gw), 1)
    tri = (jj > ss).astype(BF16)

    def page_spec(pg):
        return pl.BlockSpec((1, d_sb, page), lambda b, j, pt: (pt[b, n_pages - 1 - (j * n_pg + pg)], 0, 0))

    tok = pl.BlockSpec((1, t_new, d_sb), lambda b, j, pt: (b, 0, 0))
    grid_spec = pltpu.PrefetchScalarGridSpec(
        num_scalar_prefetch=1,
        grid=(bd, n_j),
        in_specs=[tok, tok, tok,
                  pl.BlockSpec((rows, LANES), lambda b, j, pt: (0, 0)),
                  pl.BlockSpec((gw, gw), lambda b, j, pt: (0, 0))]
                 + [page_spec(pg) for pg in range(n_pg)] * 2,
        out_specs=tok,
        scratch_shapes=[pltpu.VMEM((rows, d_sb), F32), pltpu.VMEM((rows, LANES), F32)])
    return pl.pallas_call(
        functools.partial(_s_attn_kernel, n_pg=n_pg, t_new=t_new, n_heads=n_heads),
        grid_spec=grid_spec,
        out_shape=jax.ShapeDtypeStruct((bd, t_new, d_sb), F32),
        compiler_params=_cparams(2),
        name="s_attn",
    )(page_table, q3, kn3, vn3, bias_rows, tri, *([cache_kt] * n_pg), *([cache_vt] * n_pg))


def _s_post_kernel(x_ref, sb_ref, u_ref, gv_ref, coef_ref, bt_ref, gt1_ref, sh2_ref, sc2_ref, gsb_ref, gml_ref,
                   wout_ref, g2_ref, wrt_ref, brt_ref, xm_ref, h2_ref, lg_ref, *, t_new):
    gv = gv_ref[...]
    mixed = bt_ref[...] + coef_ref[0] * gv
    for dd in range(1, t_new):
        mixed = mixed + coef_ref[dd] * pltpu.roll(gv, dd, 0)
    mlp = u_ref[...] * mixed
    _mix_tail(x_ref[...], sb_ref[...], mlp, gt1_ref[...], sh2_ref[...], sc2_ref[...], gsb_ref[...], gml_ref[...],
              wout_ref[...], g2_ref[...], wrt_ref[...], brt_ref[...], xm_ref, h2_ref, lg_ref)


def _s_post(x2, sb2, u2, gv2, coef, bias_t, gt1, sh2, sc2, gsb, gml, wout_bf, g2, wrt_bf, brt_b, t_new):
    n, d = x2.shape
    e = wrt_bf.shape[0]
    args = (x2, sb2, u2, gv2, coef, bias_t, gt1, sh2, sc2, gsb, gml, wout_bf, g2, wrt_bf, brt_b)
    full = lambda shape: pl.BlockSpec(shape, lambda i: (0,) * len(shape))
    return pl.pallas_call(
        functools.partial(_s_post_kernel, t_new=t_new),
        grid=(1,),
        in_specs=[full(a.shape) for a in args],
        out_specs=[full((n, d)), full((n * (d // LANES), LANES)), full((1, e, n))],
        out_shape=[jax.ShapeDtypeStruct((n, d), F32), jax.ShapeDtypeStruct((n * (d // LANES), LANES), F32),
                   jax.ShapeDtypeStruct((1, e, n), F32)],
        compiler_params=_cparams(1),
        name="s_post",
    )(*args)


def _route_kernel(lg_ref, tri_ref, p_ref, l_ref, n_ref, loc_ref, *, buf_rows, tile0):
    lg = lg_ref[0]
    n_e, tr = lg.shape
    eidx = lax.broadcasted_iota(I32, (n_e, tr), 0)
    work = lg
    cnt = jnp.zeros((n_e, tr), F32)
    sels, vals = [], []
    for kk in range(TOP_K):
        m = jnp.max(work, axis=0, keepdims=True)
        ek = jnp.min(jnp.where(work == m, eidx, n_e), axis=0, keepdims=True)
        sel = eidx == ek
        work = jnp.where(sel, -jnp.inf, work)
        cnt = cnt + sel.astype(F32)
        sels.append(sel)
        vals.append(m)
    ex = [jnp.exp(vv - vals[0]) for vv in vals]
    den = ex[0] + ex[1] + ex[2] + ex[3]
    for kk in range(TOP_K):
        p_ref[0, kk:kk + 1, :] = ex[kk] / den
    before = jnp.dot(cnt.astype(BF16), tri_ref[...], preferred_element_type=F32)
    n_tok = jnp.zeros((n_e, LANES), F32) + jnp.sum(cnt, axis=1, keepdims=True)
    room = jnp.floor((n_tok + (SEG_CHUNK - 1)) * (1.0 / SEG_CHUNK)) * SEG_CHUNK
    row = lax.broadcasted_iota(I32, (n_e, LANES), 0)
    incl = room
    shift = 1
    while shift < n_e:
        incl = incl + jnp.where(row >= shift, pltpu.roll(incl, shift, 0), 0.0)
        shift *= 2
    loc = incl - room
    tile = tile0 + pl.program_id(0) * pl.num_programs(1) + pl.program_id(1)
    slot_base = (lax.rem(tile, 2) * buf_rows).astype(F32)
    pos = (before + jnp.concatenate([loc] * (tr // LANES), axis=1) + slot_base) * ROW_TILE
    for kk in range(TOP_K):
        l_ref[0, kk:kk + 1, :] = jnp.sum(jnp.where(sels[kk], pos, 0.0), axis=0, keepdims=True).astype(I32)
    n_ref[0] = n_tok.astype(I32)
    loc_ref[0] = loc.astype(I32)


def _group_rows(tt, n_e):
    return TOP_K * tt + n_e * SEG_CHUNK


def _route(lgt, tt, tile0):
    b, n_e, s = lgt.shape
    n_i = s // tt
    t1 = lax.broadcasted_iota(I32, (tt, tt), 0)
    t2 = lax.broadcasted_iota(I32, (tt, tt), 1)
    tri = (t1 < t2).astype(BF16)
    tok = pl.BlockSpec((1, TOP_K, tt), lambda bb, ii: (bb * n_i + ii, 0, 0))
    tab = pl.BlockSpec((1, n_e, LANES), lambda bb, ii: (bb * n_i + ii, 0, 0))
    return pl.pallas_call(
        functools.partial(_route_kernel, buf_rows=_group_rows(tt, n_e), tile0=tile0),
        grid=(b, n_i),
        in_specs=[pl.BlockSpec((1, n_e, tt), lambda bb, ii: (bb, 0, ii)),
                  pl.BlockSpec((tt, tt), lambda bb, ii: (0, 0))],
        out_specs=[tok, tok, tab, tab],
        out_shape=[jax.ShapeDtypeStruct((b * n_i, TOP_K, tt), F32), jax.ShapeDtypeStruct((b * n_i, TOP_K, tt), I32),
                   jax.ShapeDtypeStruct((b * n_i, n_e, LANES), I32), jax.ShapeDtypeStruct((b * n_i, n_e, LANES), I32)],
        compiler_params=_cparams(2),
        name="route",
    )(lgt, tri)


def _rows(ref, row, n_rows):
    return ref.at[pl.ds(pl.multiple_of(row * ROW_TILE, ROW_TILE), n_rows * ROW_TILE)]


def _segment_copies(n_ref, loc_ref, gs_ref, tile, n_e, fn):
    def per_expert(ee, carry):
        idx = tile * n_e + ee
        n_chunks = lax.shift_right_logical(n_ref[idx] + (SEG_CHUNK - 1), SEG_CHUNK.bit_length() - 1)

        def per_chunk(j, c):
            fn(loc_ref[idx] + j * SEG_CHUNK, gs_ref[idx] + j * SEG_CHUNK)
            return c

        lax.fori_loop(0, n_chunks, per_chunk, 0)
        return carry

    lax.fori_loop(0, n_e, per_expert, 0)


def _dispatch_kernel(n_ref, loc_ref, gs_ref, sz_ref, pd_ref, ps_ref, nu_ref, l_ref, hp_ref, hs_ref, xs_ref, srt,
                     zero_ref, sem, zsem, *, tt, n_e, bm, n_blocks, n_p):
    tile = pl.program_id(0)
    slot = lax.rem(tile, 2)
    buf_rows = _group_rows(tt, n_e)

    def fill_zero(first_row, n_rows):
        n_chunks = lax.shift_right_logical(n_rows + (SEG_CHUNK - 1), SEG_CHUNK.bit_length() - 1)

        def chunk(j):
            return pltpu.make_async_copy(zero_ref, _rows(xs_ref, first_row + j * SEG_CHUNK, SEG_CHUNK), zsem)

        lax.fori_loop(0, n_chunks, lambda j, c: (chunk(j).start(), c)[1], 0)
        lax.fori_loop(0, n_chunks, lambda j, c: (chunk(j).wait(), c)[1], 0)

    @pl.when(tile == 0)
    def _():
        zero_ref[...] = jnp.zeros_like(zero_ref)
        srt[...] = jnp.zeros_like(srt)

        def per_expert(ee, carry):
            fill_zero(ps_ref[ee] + sz_ref[ee], pd_ref[ee] - sz_ref[ee])
            return carry

        lax.fori_loop(0, n_e, per_expert, 0)
        fill_zero(nu_ref[0] * bm, (n_blocks - nu_ref[0]) * bm)

    def scatter_from(h_ref):
        def scatter(t8, carry):
            base = t8 * SUBLANES
            rows = [h_ref[pl.ds(pl.multiple_of((base + u) * ROW_TILE, ROW_TILE), ROW_TILE), :]
                    for u in range(SUBLANES)]
            for u in range(SUBLANES):
                for kk in range(TOP_K):
                    srt[pl.ds(pl.multiple_of(l_ref[0, kk, base + u], ROW_TILE), ROW_TILE), :] = rows[u]
            return carry

        lax.fori_loop(0, tt // SUBLANES, scatter, 0)

    pl.when(tile < n_p)(lambda: scatter_from(hp_ref))
    pl.when(tile >= n_p)(lambda: scatter_from(hs_ref))

    def copy(s):
        return lambda lrow, grow: pltpu.make_async_copy(_rows(srt, s * buf_rows + lrow, SEG_CHUNK),
                                                        _rows(xs_ref, grow, SEG_CHUNK), sem.at[s])

    @pl.when(tile > 0)
    def _():
        mk = copy(1 - slot)
        _segment_copies(n_ref, loc_ref, gs_ref, tile - 1, n_e, lambda lrow, grow: mk(lrow, grow).wait())

    mk = copy(slot)
    _segment_copies(n_ref, loc_ref, gs_ref, tile, n_e, lambda lrow, grow: mk(lrow, grow).start())

    @pl.when(tile == pl.num_programs(0) - 1)
    def _():
        _segment_copies(n_ref, loc_ref, gs_ref, tile, n_e, lambda lrow, grow: mk(lrow, grow).wait())


def _dispatch(tables, l_t, h2_p, h2_s, n_blocks, bm):
    n_tiles, kk, tt = l_t.shape
    n_p = n_tiles - 1
    n_e = tables[3].shape[0]
    grid_spec = pltpu.PrefetchScalarGridSpec(
        num_scalar_prefetch=7,
        grid=(n_tiles,),
        in_specs=[pl.BlockSpec((1, kk, tt), lambda t, *_: (t, 0, 0), memory_space=pltpu.SMEM),
                  pl.BlockSpec((tt * ROW_TILE, LANES), lambda t, *_: (jnp.minimum(t, n_p - 1), 0)),
                  pl.BlockSpec((tt * ROW_TILE, LANES), lambda t, *_: (0, 0))],
        out_specs=pl.BlockSpec(memory_space=pl.ANY),
        scratch_shapes=[pltpu.VMEM((2 * _group_rows(tt, n_e) * ROW_TILE, LANES), F32),
                        pltpu.VMEM((SEG_CHUNK * ROW_TILE, LANES), F32),
                        pltpu.SemaphoreType.DMA((2,)), pltpu.SemaphoreType.DMA(())])
    return pl.pallas_call(
        functools.partial(_dispatch_kernel, tt=tt, n_e=n_e, bm=bm, n_blocks=n_blocks, n_p=n_p),
        grid_spec=grid_spec,
        out_shape=jax.ShapeDtypeStruct((n_blocks * bm * ROW_TILE, LANES), F32),
        compiler_params=_cparams(1),
        name="dispatch",
    )(*tables, l_t, h2_p, h2_s)


def _expert_kernel(be_ref, nu_ref, xs_ref, wgu_ref, bgu_ref, wd_ref, bd_ref, yb_ref, *, bm, d, d_ff):
    blk = pl.program_id(0)
    n_c = d // LANES

    @pl.when(blk < nu_ref[0])
    def _():
        x = jnp.concatenate([xs_ref[pl.ds(c, bm, stride=n_c), :] for c in range(n_c)], axis=1)
        gu = jnp.dot(x.astype(BF16), wgu_ref[0].astype(BF16), preferred_element_type=F32) + bgu_ref[0]
        gate = jnp.minimum(gu[:, :d_ff], SWIGLU_LIMIT)
        up = jnp.clip(gu[:, d_ff:], -SWIGLU_LIMIT, SWIGLU_LIMIT)
        act = gate * (1.0 / (1.0 + jnp.exp(-SWIGLU_ALPHA * gate))) * (up + 1.0)
        y = jnp.dot(act.astype(BF16), wd_ref[0].astype(BF16), preferred_element_type=F32) + bd_ref[0]
        for c in range(n_c):
            yb_ref[pl.ds(c, bm, stride=n_c), :] = y[:, c * LANES:(c + 1) * LANES]

    @pl.when(blk >= nu_ref[0])
    def _():
        yb_ref[...] = jnp.zeros_like(yb_ref)


def _experts(block_e, n_used, xs_rows, wgu, bgu, wd, bd, n_blocks, bm):
    n_e, d, d_ff2 = wgu.shape
    d_ff = d_ff2 // 2
    grid_spec = pltpu.PrefetchScalarGridSpec(
        num_scalar_prefetch=2,
        grid=(n_blocks,),
        in_specs=[pl.BlockSpec((bm * ROW_TILE, LANES), lambda i, be, nu: (i, 0)),
                  pl.BlockSpec((1, d, d_ff2), lambda i, be, nu: (be[i], 0, 0)),
                  pl.BlockSpec((1, 1, d_ff2), lambda i, be, nu: (be[i], 0, 0)),
                  pl.BlockSpec((1, d_ff, d), lambda i, be, nu: (be[i], 0, 0)),
                  pl.BlockSpec((1, 1, d), lambda i, be, nu: (be[i], 0, 0))],
        out_specs=pl.BlockSpec((bm * ROW_TILE, LANES), lambda i, be, nu: (i, 0)))
    return pl.pallas_call(
        functools.partial(_expert_kernel, bm=bm, d=d, d_ff=d_ff),
        grid_spec=grid_spec,
        out_shape=jax.ShapeDtypeStruct((n_blocks * bm * ROW_TILE, LANES), F32),
        compiler_params=_cparams(1),
        name="experts",
    )(block_e, n_used, xs_rows, wgu, bgu.reshape(n_e, 1, d_ff2), wd, bd.reshape(n_e, 1, d))


def _combine_kernel(n_ref, loc_ref, gs_ref, l_ref, p_ref, xmp_ref, gtp_ref, xms_ref, gts_ref, gf_ref, yb_ref,
                    yp_ref, ys_ref, gat, ytile, sem, *, tt, n_e, d, n_p):
    tile = pl.program_id(0)
    slot = lax.rem(tile, 2)
    buf_rows = _group_rows(tt, n_e)
    n_c = d // LANES

    def copy(s):
        return lambda lrow, grow: pltpu.make_async_copy(_rows(yb_ref, grow, SEG_CHUNK),
                                                        _rows(gat, s * buf_rows + lrow, SEG_CHUNK), sem.at[s])

    @pl.when(tile == 0)
    def _():
        mk = copy(slot)
        _segment_copies(n_ref, loc_ref, gs_ref, tile, n_e, lambda lrow, grow: mk(lrow, grow).start())

    @pl.when(tile + 1 < pl.num_programs(0))
    def _():
        mk = copy(1 - slot)
        _segment_copies(n_ref, loc_ref, gs_ref, tile + 1, n_e, lambda lrow, grow: mk(lrow, grow).start())

    mk = copy(slot)
    _segment_copies(n_ref, loc_ref, gs_ref, tile, n_e, lambda lrow, grow: mk(lrow, grow).wait())

    def gather(t8, carry):
        base = t8 * SUBLANES
        for u in range(SUBLANES):
            t = base + u
            acc = None
            for kk in range(TOP_K):
                term = p_ref[0, kk, t] * gat[pl.ds(pl.multiple_of(l_ref[0, kk, t], ROW_TILE), ROW_TILE), :]
                acc = term if acc is None else acc + term
            ytile[pl.ds(pl.multiple_of(t * ROW_TILE, ROW_TILE), ROW_TILE), :] = acc
        return carry

    lax.fori_loop(0, tt // SUBLANES, gather, 0)
    def finish(xm_ref, gt_ref, y_ref):
        y = jnp.concatenate([ytile[pl.ds(c, tt, stride=n_c), :] for c in range(n_c)], axis=1)
        xo = xm_ref[0] + gt_ref[0] * y
        y_ref[0] = _rms(xo) * gf_ref[...]

    pl.when(tile < n_p)(lambda: finish(xmp_ref, gtp_ref, yp_ref))
    pl.when(tile >= n_p)(lambda: finish(xms_ref, gts_ref, ys_ref))


def _combine(tables, l_t, p_t, xm_p, gt_p, xm_s, gt_s, gfin, yb_rows):
    n_tiles, kk, tt = l_t.shape
    n_p = n_tiles - 1
    b, s, d = xm_p.shape
    n_i = s // tt
    n_e = tables[0].shape[0] // n_tiles
    tp = lambda t: jnp.minimum(t, n_p - 1)
    tok_smem = pl.BlockSpec((1, kk, tt), lambda t, *_: (t, 0, 0), memory_space=pltpu.SMEM)
    p_blk = pl.BlockSpec((1, tt, d), lambda t, *_: (tp(t) // n_i, tp(t) % n_i, 0))
    s_blk = pl.BlockSpec((1, tt, d), lambda t, *_: (0, 0, 0))
    grid_spec = pltpu.PrefetchScalarGridSpec(
        num_scalar_prefetch=3,
        grid=(n_tiles,),
        in_specs=[tok_smem, tok_smem, p_blk,
                  pl.BlockSpec((1, 1, d), lambda t, *_: (tp(t) // n_i, 0, 0)),
                  s_blk, s_blk,
                  pl.BlockSpec((1, d), lambda t, *_: (0, 0)),
                  pl.BlockSpec(memory_space=pl.ANY)],
        out_specs=[p_blk, s_blk],
        scratch_shapes=[pltpu.VMEM((2 * _group_rows(tt, n_e) * ROW_TILE, LANES), F32),
                        pltpu.VMEM((tt * ROW_TILE, LANES), F32),
                        pltpu.SemaphoreType.DMA((2,))])
    return pl.pallas_call(
        functools.partial(_combine_kernel, tt=tt, n_e=n_e, d=d, n_p=n_p),
        grid_spec=grid_spec,
        out_shape=[jax.ShapeDtypeStruct((b, s, d), F32), jax.ShapeDtypeStruct(xm_s.shape, F32)],
        compiler_params=_cparams(1),
        name="combine",
    )(*tables[:3], l_t, p_t, xm_p, gt_p, xm_s, gt_s, gfin, yb_rows)


def _moe(lg_p, h2_p, xm_p, gt_p, lg_s, h2_s, xm_s, gt_s, gfin, wgu, bgu, wd, bd):
    b, n_e, s = lg_p.shape
    tt = min(TT_MOE, s)
    assert lg_s.shape == (1, n_e, tt), "the sample group must be exactly one routing tile"
    n_p = b * (s // tt)
    n_assign = (n_p + 1) * tt * TOP_K
    bm = BM_EXPERT if n_assign >= n_e * BM_EXPERT else BM_EXPERT_SMALL
    n_blocks = -(-(n_assign + n_e * (SEG_CHUNK - 1)) // bm) + n_e + 1
    routed = [_route(lg_p, tt, 0), _route(lg_s, tt, n_p)]
    p_t, l_t, n_tile, loc_tile = [jnp.concatenate([rp, rs], axis=0) for rp, rs in zip(*routed)]
    n2 = n_tile[:, :, 0]
    sizes = jnp.sum(n2, axis=0)
    padded = ((sizes + (SEG_CHUNK - 1) + bm - 1) // bm) * bm
    pad_end = jnp.cumsum(padded)
    pad_start = pad_end - padded
    g_start = pad_start[None, :] + jnp.cumsum(n2, axis=0) - n2
    block_start = jnp.arange(n_blocks, dtype=I32) * bm
    block_e = jnp.minimum(jnp.sum((pad_end[None, :] <= block_start[:, None]).astype(I32), axis=1), n_e - 1)
    n_used = (pad_end[-1:] // bm).astype(I32)
    tables = (n2.reshape(-1), loc_tile[:, :, 0].reshape(-1), g_start.reshape(-1).astype(I32), sizes, padded,
              pad_start, n_used)
    xs_rows = _dispatch(tables, l_t, h2_p, h2_s, n_blocks, bm)
    yb_rows = _experts(block_e, n_used, xs_rows, wgu, bgu, wd, bd, n_blocks, bm)
    return _combine(tables, l_t, p_t, xm_p, gt_p, xm_s, gt_s, gfin, yb_rows)


def kernel(x_prompt, x_sample, cache_k, cache_v, page_table, c_prompt, c_sample, w_ada, b_ada, g_norm_mix,
           g_norm_ffn, w_in, b_sb, w_s, b_s, g_sb_out, g_mlp_out, w_out, w_router, b_router, w_gate_up,
           b_gate_up, w_down, b_down, g_final):
    b, s, d = x_prompt.shape
    bd, t_new, _ = x_sample.shape
    depth, n_pool, page, h_sb, d_head = cache_k.shape
    assert depth == 1 and d_head == D_HEAD
    d_sb = h_sb * d_head
    d_mlp = g_mlp_out.shape[1]
    n_e = w_router.shape[2]
    n_tok_s = bd * t_new

    win_bf = w_in[0].astype(BF16)
    wout_bf = w_out[0].astype(BF16)
    wada_bf = w_ada[0].astype(BF16)
    wrt_bf = jnp.transpose(w_router[0]).astype(BF16)
    brt_b = jnp.broadcast_to(b_router[0][:, None], (n_e, LANES))
    g1 = g_norm_mix[0].reshape(1, d)
    g2 = g_norm_ffn[0].reshape(1, d)
    gsb = g_sb_out[0].reshape(1, d_sb)
    gml = g_mlp_out[0].reshape(1, d_mlp)
    gfin = g_final.reshape(1, d)
    bs_full = jnp.repeat(jnp.transpose(b_s[0]), d_head, axis=1)
    bias_rows = jnp.broadcast_to(jnp.repeat(b_sb[0], t_new)[:, None], (h_sb * t_new, LANES))

    ada = _ada(jnp.concatenate([c_prompt, c_sample], axis=0), wada_bf, b_ada[0])
    ada_p = ada[:b].reshape(b, N_ADA, d)
    ada_s = jnp.repeat(ada[b:].reshape(bd, N_ADA, d), t_new, axis=0)

    k_p, v_p, xm_p, h2_p, lg_p = _mix_prompt(x_prompt, ada_p, b_sb[0], g1, win_bf, w_s[0], bs_full, gsb, gml,
                                             wout_bf, g2, wrt_bf, brt_b, page)

    x_s2 = x_sample.reshape(n_tok_s, d)
    q_s, k_s, v_s, u_s, gv_s = _s_pre(x_s2, ada_s[:, 0], ada_s[:, 1], g1, win_bf, d_sb, d_mlp)
    cache_kt = jnp.transpose(cache_k[0], (0, 2, 3, 1)).reshape(n_pool, d_sb, page)
    cache_vt = jnp.transpose(cache_v[0], (0, 2, 3, 1)).reshape(n_pool, d_sb, page)
    sb_s = _s_attn(q_s.reshape(bd, t_new, d_sb), k_s.reshape(bd, t_new, d_sb), v_s.reshape(bd, t_new, d_sb),
                   bias_rows, cache_kt, cache_vt, page_table)
    w_tril = jnp.tril(w_s[0][:, :t_new, :t_new])
    tt = jnp.arange(t_new)
    coef = jnp.stack([jnp.where((tt >= dd)[None, :], w_tril[:, tt, jnp.maximum(tt - dd, 0)], 0.0)
                      for dd in range(t_new)])
    coef = jnp.tile(jnp.repeat(jnp.transpose(coef, (0, 2, 1)), d_head, axis=2), (1, bd, 1))
    bias_t = jnp.tile(jnp.repeat(jnp.transpose(b_s[0][:, :t_new]), d_head, axis=1), (bd, 1))
    xm_s, h2_s, lg_s = _s_post(x_s2, sb_s.reshape(n_tok_s, d_sb), u_s, gv_s, coef, bias_t, ada_s[:, 2],
                               ada_s[:, 3], ada_s[:, 4], gsb, gml, wout_bf, g2, wrt_bf, brt_b, t_new)

    y_prompt, y_sample = _moe(lg_p, h2_p, xm_p, ada_p[:, 5:6, :], lg_s, h2_s, xm_s.reshape(1, n_tok_s, d),
                              ada_s[:, 5].reshape(1, n_tok_s, d), gfin, w_gate_up[0], b_gate_up[0], w_down[0],
                              b_down[0])

    return (y_prompt, y_sample.reshape(bd, t_new, d),
            jnp.transpose(k_p.reshape(b, s // page, h_sb, d_head, page), (0, 1, 4, 2, 3))[None],
            jnp.transpose(v_p.reshape(b, s // page, h_sb, d_head, page), (0, 1, 4, 2, 3))[None],
            k_s.reshape(1, bd, t_new, h_sb, d_head), v_s.reshape(1, bd, t_new, h_sb, d_head),
            gv_s.reshape(1, bd, t_new, d_mlp))
```
